```python
import math
import jax, jax.numpy as jnp
from jax import lax
import numpy as np

D_MODEL = 2048
BATCH = 32
SEQ = 256
DEPTH = 2
DEC_BATCH = 8
DEC_SEQ = 1024
PAST_LEN = 256

GRID_W = 64
D_MIX = D_MODEL
N_MIXERS = 4
D_BRANCH = D_MIX // N_MIXERS
N_FFT_GROUPS = 4
FFT_GROUP = D_BRANCH // N_FFT_GROUPS
POOL_WINDOWS = (2, 4, 8, 16)
POOL_GROUP = D_BRANCH // len(POOL_WINDOWS)
N_ATT_HEADS = 4
DIFF_HEAD = D_BRANCH // (2 * N_ATT_HEADS)
QK_DIM = 2 * DIFF_HEAD
V_DIM = 2 * DIFF_HEAD
N_FREQ = DIFF_HEAD // 4
CONV_WIDTH = 31
ROPE_BASE = 10000.0
EPS = 1e-6
Q_BLOCK = 128
N_IN_CHUNKS = 11
D_IN_PROJ = N_IN_CHUNKS * D_BRANCH

kernel_name = 'hybrid_diffusion_parallel_mixer_step'


def rmsnorm(x, g):
    xf = x.astype(jnp.float32)
    y = xf * lax.rsqrt(jnp.mean(xf * xf, axis=-1, keepdims=True) + EPS)
    return (y * g.astype(jnp.float32)).astype(x.dtype)


def layernorm(x, g, b):
    xf = x.astype(jnp.float32)
    mu = jnp.mean(xf, axis=-1, keepdims=True)
    var = jnp.mean(jnp.square(xf - mu), axis=-1, keepdims=True)
    y = (xf - mu) * lax.rsqrt(var + EPS)
    return (y * g.astype(jnp.float32) + b.astype(jnp.float32)).astype(x.dtype)


def axial_rope_tables(rows):
    t_row = jnp.repeat(jnp.arange(rows), GRID_W).astype(jnp.float32)
    t_col = jnp.tile(jnp.arange(GRID_W), rows).astype(jnp.float32)
    inv = ROPE_BASE ** (-jnp.arange(N_FREQ, dtype=jnp.float32) / N_FREQ)
    ang = jnp.stack([t_row[:, None] * inv, t_col[:, None] * inv], axis=1)
    return jnp.cos(ang), jnp.sin(ang)


def apply_rope(x, cos, sin):
    xs = x.astype(jnp.float32).reshape(x.shape[:-1] + (2, 2, N_FREQ))
    x1, x2 = xs[..., 0, :], xs[..., 1, :]
    cs, sn = cos[:, None], sin[:, None]
    y = jnp.stack([x1 * cs - x2 * sn, x2 * cs + x1 * sn], axis=-2)
    return y.reshape(x.shape).astype(x.dtype)


def fourier_mix(u, w_f):
    b, s, _ = u.shape
    uf = u.astype(jnp.float32).reshape(b, s, N_FFT_GROUPS, FFT_GROUP)
    mixed = jnp.fft.fft2(uf, axes=(1, 3), norm='ortho').real
    return mixed.reshape(b, s, D_BRANCH).astype(u.dtype) @ w_f


def pool_mix(u, w_pool, pool_scale):
    b, s, _ = u.shape
    uf = u.astype(jnp.float32)
    cs = jnp.pad(jnp.cumsum(uf, axis=1), ((0, 0), (1, 0), (0, 0)))
    t = jnp.arange(s)
    outs = []
    for gi, w in enumerate(POOL_WINDOWS):
        lo = jnp.clip(t - w // 2, 0, s)
        hi = jnp.clip(t - w // 2 + w, 0, s)
        sl = slice(gi * POOL_GROUP, (gi + 1) * POOL_GROUP)
        csg = cs[..., sl]
        mean = (csg[:, hi] - csg[:, lo]) / (hi - lo).astype(jnp.float32)[:, None]
        outs.append(mean - uf[..., sl])
    pooled = jnp.stack(outs, axis=2).astype(u.dtype)
    mixed = jnp.einsum('bsgc,gcd->bsgd', pooled, w_pool)
    return mixed.reshape(b, s, D_BRANCH) * pool_scale


def conv_module(a, g_glu, dw, dw_b, ln_g, ln_b, w_pw):
    u = a * jax.nn.sigmoid(g_glu)
    pad = CONV_WIDTH // 2
    y = lax.conv_general_dilated(u, dw[:, None, :].astype(u.dtype), window_strides=(1,),
                                 padding=[(pad, pad)], dimension_numbers=('NWC', 'WIO', 'NWC'),
                                 feature_group_count=D_BRANCH) + dw_b
    y = jax.nn.silu(layernorm(y, ln_g, ln_b))
    return y @ w_pw


def diff_attention(q, k, v, lam):
    b, h, sq = q.shape[:3]
    nblk = sq // Q_BLOCK
    qb = jnp.moveaxis(q.reshape(b, h, nblk, Q_BLOCK, 2, DIFF_HEAD), 2, 0)

    def block(qi):
        s = jnp.einsum('bhqmd,bhkmd->bhmqk', qi, k,
                       preferred_element_type=jnp.float32) * (DIFF_HEAD ** -0.5)
        p = jax.nn.softmax(s, axis=-1)
        att = p[:, :, 0] - lam * p[:, :, 1]
        return jnp.einsum('bhqk,bhkv->bhqv', att.astype(v.dtype), v)

    ob = lax.map(block, qb)
    return jnp.moveaxis(ob, 0, 2).reshape(b, h, sq, V_DIM)


def trunk_layer(x, cvec, l, p, rope=None, ctx_k=None, ctx_v=None):
    b, s, _ = x.shape
    shift, scale, gate = jnp.split(jax.nn.silu(cvec) @ p['w_mod'][l] + p['b_mod'][l], 3, axis=-1)
    h = rmsnorm(x, p['norm_g'][l]) * (1 + scale) + shift
    (f_x, f_g, p_x, p_g, q, k, v, a_g, c_a, c_b, c_g) = jnp.split(h @ p['w_in'][l], N_IN_CHUNKS, axis=-1)
    y_f = fourier_mix(f_x, p['w_fourier'][l]) * jax.nn.silu(f_g)
    y_p = pool_mix(p_x, p['w_pool'][l], p['pool_scale'][l]) * jax.nn.silu(p_g)
    y_c = conv_module(c_a, c_b, p['conv_dw'][l], p['conv_dw_b'][l], p['conv_ln_g'][l],
                      p['conv_ln_b'][l], p['w_conv_pw'][l]) * jax.nn.silu(c_g)
    q = q.reshape(b, s, N_ATT_HEADS, 2, DIFF_HEAD).transpose(0, 2, 1, 3, 4)
    k = k.reshape(b, s, N_ATT_HEADS, 2, DIFF_HEAD).transpose(0, 2, 1, 3, 4)
    v = v.reshape(b, s, N_ATT_HEADS, V_DIM).transpose(0, 2, 1, 3)
    lam_init = 0.8 - 0.6 * math.exp(-0.3 * l)
    lq1, lk1, lq2, lk2 = p['diff_lambda'][l].astype(jnp.float32)
    lam = jnp.exp(jnp.sum(lq1 * lk1)) - jnp.exp(jnp.sum(lq2 * lk2)) + lam_init
    if rope is None:
        o = diff_attention(q, k, v, lam)
    else:
        cos, sin = rope
        k_all = jnp.concatenate([ctx_k, apply_rope(k, cos, sin)], axis=2)
        v_all = jnp.concatenate([ctx_v, v], axis=2)
        o = diff_attention(apply_rope(q, cos, sin), k_all, v_all, lam)
    o = rmsnorm(o, p['subln_g'][l]) * (1 - lam_init)
    y_a = o.transpose(0, 2, 1, 3).reshape(b, s, D_BRANCH) * jax.nn.silu(a_g)
    y = jnp.concatenate([y_f, y_p, y_a, y_c], axis=-1) @ p['w_out'][l]
    return x + gate * y, k, v


def setup_inputs(seed: int = 0) -> dict:
    key = jax.random.key(seed)
    ks = jax.random.split(key, 24)
    f32 = jnp.float32

    def nrm(k, shape, scale=1.0):
        return jax.random.normal(k, shape, f32) * scale

    return {
        'x_prompt': nrm(ks[0], (BATCH, SEQ, D_MODEL)),
        'x_sample': nrm(ks[1], (DEC_BATCH, DEC_SEQ, D_MODEL)),
        'cache_k': nrm(ks[2], (DEC_BATCH, DEPTH, N_ATT_HEADS, PAST_LEN, QK_DIM)),
        'cache_v': nrm(ks[3], (DEC_BATCH, DEPTH, N_ATT_HEADS, PAST_LEN, V_DIM)),
        'c': nrm(ks[4], (DEC_BATCH, D_MODEL)),
        'c_ctx': nrm(ks[5], (D_MODEL,)),
        'norm_g': 1.0 + nrm(ks[6], (DEPTH, D_MODEL), 0.02),
        'w_mod': nrm(ks[7], (DEPTH, D_MODEL, 3 * D_MODEL), 0.3 * D_MODEL ** -0.5),
        'b_mod': nrm(ks[8], (DEPTH, 3 * D_MODEL), 0.02),
        'w_in': nrm(ks[9], (DEPTH, D_MODEL, D_IN_PROJ), D_MODEL ** -0.5),
        'w_fourier': nrm(ks[10], (DEPTH, D_BRANCH, D_BRANCH), D_BRANCH ** -0.5),
        'w_pool': nrm(ks[11], (DEPTH, len(POOL_WINDOWS), POOL_GROUP, POOL_GROUP), POOL_GROUP ** -0.5),
        'pool_scale': 1.0 + nrm(ks[12], (DEPTH, D_BRANCH), 0.02),
        'diff_lambda': nrm(ks[13], (DEPTH, 4, DIFF_HEAD), 0.1),
        'subln_g': 1.0 + nrm(ks[14], (DEPTH, V_DIM), 0.02),
        'conv_dw': nrm(ks[15], (DEPTH, CONV_WIDTH, D_BRANCH), CONV_WIDTH ** -0.5),
        'conv_dw_b': nrm(ks[16], (DEPTH, D_BRANCH), 0.02),
        'conv_ln_g': 1.0 + nrm(ks[17], (DEPTH, D_BRANCH), 0.02),
        'conv_ln_b': nrm(ks[18], (DEPTH, D_BRANCH), 0.02),
        'w_conv_pw': nrm(ks[19], (DEPTH, D_BRANCH, D_BRANCH), D_BRANCH ** -0.5),
        'w_out': nrm(ks[20], (DEPTH, D_MIX, D_MODEL), D_MIX ** -0.5),
        'final_g': 1.0 + nrm(ks[21], (D_MODEL,), 0.02),
    }


def reference(x_prompt, x_sample, cache_k, cache_v, c, c_ctx, norm_g, w_mod, b_mod, w_in,
              w_fourier, w_pool, pool_scale, diff_lambda, subln_g, conv_dw, conv_dw_b,
              conv_ln_g, conv_ln_b, w_conv_pw, w_out, final_g):
    p = {'norm_g': norm_g, 'w_mod': w_mod, 'b_mod': b_mod, 'w_in': w_in, 'w_fourier': w_fourier,
         'w_pool': w_pool, 'pool_scale': pool_scale, 'diff_lambda': diff_lambda,
         'subln_g': subln_g, 'conv_dw': conv_dw, 'conv_dw_b': conv_dw_b, 'conv_ln_g': conv_ln_g,
         'conv_ln_b': conv_ln_b, 'w_conv_pw': w_conv_pw, 'w_out': w_out}

    x = x_prompt
    ctx_keys, ctx_vals = [], []
    for l in range(DEPTH):
        x, k, v = trunk_layer(x, c_ctx, l, p)
        ctx_keys.append(k.reshape(k.shape[:3] + (QK_DIM,)))
        ctx_vals.append(v)
    y_prompt = rmsnorm(x, final_g)
    new_cache_k = jnp.stack(ctx_keys, axis=1)
    new_cache_v = jnp.stack(ctx_vals, axis=1)

    rows = x_sample.shape[1] // GRID_W
    rope = axial_rope_tables(rows)
    x = x_sample
    for l in range(DEPTH):
        ck = cache_k[:, l]
        ctx_k = ck.reshape(ck.shape[:3] + (2, DIFF_HEAD))
        x, _, _ = trunk_layer(x, c[:, None, :], l, p, rope, ctx_k, cache_v[:, l])
    y_sample = rmsnorm(x, final_g)
    return (y_prompt, y_sample, new_cache_k, new_cache_v)
```

```python
import functools
import math

import numpy as np
import jax
import jax.numpy as jnp
from jax import lax
from jax.experimental import pallas as pl
from jax.experimental.pallas import tpu as pltpu

D_MODEL = 2048
BATCH = 32
SEQ = 256
DEPTH = 2
DEC_BATCH = 8
DEC_SEQ = 1024
PAST_LEN = 256
GRID_W = 64
D_BRANCH = 512
FFT_GROUP = 128
POOL_WINDOWS = (2, 4, 8, 16)
POOL_GROUP = 128
N_ATT_HEADS = 4
DIFF_HEAD = 64
HEAD_W = 128
N_FREQ = 16
CONV_WIDTH = 31
ROPE_BASE = 10000.0
EPS = 1e-6
N_IN_CHUNKS = 11
D_IN_PROJ = N_IN_CHUNKS * D_BRANCH
MOD_ROWS = 16

W_FX, W_FG, W_PX, W_PG, W_Q, W_K, W_V, W_AG, W_CA, W_CB, W_CG = range(N_IN_CHUNKS)
N_ACT_CHUNKS = 10
D_ACT = N_ACT_CHUNKS * D_BRANCH
A_FX, A_FG, A_PX, A_PG, A_Q, A_K, A_V, A_AG, A_GLU, A_CG = (i * D_BRANCH for i in range(N_ACT_CHUNKS))

Q_SCALE = DIFF_HEAD ** -0.5 * math.log2(math.e)

ROW_BLOCK = 256
POOL_HALO = 128
CONV_PAD = 16
CONV_ROWS = 32
CONV_WIN = ROW_BLOCK + 24
TOKEN_TILE = 512
VMEM_LIMIT = 56 * 1024 * 1024

F32 = jnp.float32
BF16 = jnp.bfloat16


def _sigmoid(x):
    return 0.5 * jnp.tanh(0.5 * x) + 0.5


def _silu(x):
    return x * _sigmoid(x)


def _resident(block_shape, index_map):
    return pl.BlockSpec(block_shape, index_map, pipeline_mode=pl.Buffered(1))


@functools.lru_cache(maxsize=None)
def _dft_positions(s):
    k = np.arange(s, dtype=np.int64)
    ang = 2.0 * np.pi * ((k[:, None] * k[None, :]) % s) / s
    return (np.concatenate([np.cos(ang), np.sin(ang)], axis=0) / np.sqrt(s)).astype(np.float32)


@functools.lru_cache(maxsize=None)
def _dft_channels():
    k = np.arange(FFT_GROUP, dtype=np.int64)
    ang = 2.0 * np.pi * ((k[:, None] * k[None, :]) % FFT_GROUP) / FFT_GROUP
    return (np.stack([np.cos(ang), -np.sin(ang)]) / np.sqrt(FFT_GROUP)).astype(np.float32)


@functools.lru_cache(maxsize=None)
def _pool_windows():
    i = np.arange(ROW_BLOCK)[:, None]
    j = np.arange(2 * ROW_BLOCK)[None, :] - POOL_HALO
    return np.stack([((j >= i - w // 2) & (j < i + w // 2)) for w in POOL_WINDOWS]).astype(np.float32)


@functools.lru_cache(maxsize=None)
def _rope_tables(s):
    rows = s // GRID_W
    t_row = np.repeat(np.arange(rows), GRID_W).astype(np.float64)
    t_col = np.tile(np.arange(GRID_W), rows).astype(np.float64)
    inv = ROPE_BASE ** (-np.arange(N_FREQ, dtype=np.float64) / N_FREQ)
    lane = np.arange(HEAD_W)
    d = lane % DIFF_HEAD
    axis = d // (2 * N_FREQ)
    half = (d % (2 * N_FREQ)) // N_FREQ
    freq = d % N_FREQ
    pos = np.where(axis[None, :] == 0, t_row[:, None], t_col[:, None])
    ang = pos * inv[freq][None, :]
    cos, sin = np.cos(ang), np.sin(ang)
    sin_a = np.where(half[None, :] == 0, -sin, 0.0)
    sin_b = np.where(half[None, :] == 1, sin, 0.0)
    return cos.astype(np.float32), sin_a.astype(np.float32), sin_b.astype(np.float32)


def _fold_kernel(dft_ref, wf_ref, o_ref):
    for part in range(2):
        for g in range(D_BRANCH // FFT_GROUP):
            rows = slice(g * FFT_GROUP, (g + 1) * FFT_GROUP)
            prod = jnp.dot(dft_ref[part], wf_ref[rows, :], preferred_element_type=F32,
                           precision=lax.Precision.HIGHEST)
            o_ref[part * D_BRANCH + g * FFT_GROUP:part * D_BRANCH + (g + 1) * FFT_GROUP, :] = prod.astype(BF16)


def _fold_fourier(w_fourier):
    dft = jnp.asarray(_dft_channels())
    return pl.pallas_call(
        _fold_kernel,
        out_shape=jax.ShapeDtypeStruct((DEPTH, 2 * D_BRANCH, D_BRANCH), BF16),
        grid=(DEPTH,),
        in_specs=[pl.BlockSpec((2, FFT_GROUP, FFT_GROUP), lambda l: (0, 0, 0)),
                  pl.BlockSpec((None, D_BRANCH, D_BRANCH), lambda l: (l, 0, 0))],
        out_specs=pl.BlockSpec((None, 2 * D_BRANCH, D_BRANCH), lambda l: (l, 0, 0)),
        name="fold_fourier",
    )(dft, w_fourier)


def _mod_kernel(cv_ref, w_ref, b_ref, o_ref):
    a = _silu(cv_ref[...]).astype(BF16)
    o_ref[...] = jnp.dot(a, w_ref[...].astype(BF16), preferred_element_type=F32) + b_ref[...]


def _modulation(cv, w_mod, b_mod):
    tn = 1024
    return pl.pallas_call(
        _mod_kernel,
        out_shape=jax.ShapeDtypeStruct((DEPTH, MOD_ROWS, 3 * D_MODEL), F32),
        grid=(DEPTH, 3 * D_MODEL // tn),
        in_specs=[pl.BlockSpec((MOD_ROWS, D_MODEL), lambda l, j: (0, 0)),
                  pl.BlockSpec((None, D_MODEL, tn), lambda l, j: (l, 0, j)),
                  pl.BlockSpec((None, 1, tn), lambda l, j: (l, 0, j))],
        out_specs=pl.BlockSpec((None, MOD_ROWS, tn), lambda l, j: (l, 0, j)),
        compiler_params=pltpu.CompilerParams(vmem_limit_bytes=VMEM_LIMIT),
        name="modulation",
    )(cv, w_mod, b_mod.reshape(DEPTH, 1, 3 * D_MODEL))


def _mod_row(prompt, tile_rows):
    if prompt:
        return lambda i: 0
    return lambda i: 1 + (i * tile_rows) // DEC_SEQ


def _rope(x, cos, sin_a, sin_b):
    return x * cos + pltpu.roll(x, HEAD_W - N_FREQ, 1) * sin_a + pltpu.roll(x, N_FREQ, 1) * sin_b


def _inproj_kernel(*refs, prompt, aliased_caches):
    it = iter(refs)
    x_ref, mod_ref, g_ref, w_ref = (next(it) for _ in range(4))
    if prompt:
        if aliased_caches:
            next(it), next(it)
        u_ref, kc_ref, vc_ref = (next(it) for _ in range(3))
    else:
        cos_ref, sa_ref, sb_ref, u_ref = (next(it) for _ in range(4))
    h_ref = next(it)

    tm = x_ref.shape[0]
    shift = mod_ref[:, 0:D_MODEL]
    scale = mod_ref[:, D_MODEL:2 * D_MODEL]
    for r in range(tm // 128):
        rows = slice(r * 128, (r + 1) * 128)
        x = x_ref[rows, :]
        y = x * lax.rsqrt(jnp.mean(x * x, axis=-1, keepdims=True) + EPS) * g_ref[...]
        h_ref[rows, :] = (y * (1.0 + scale) + shift).astype(BF16)

    def proj(c):
        return jnp.dot(h_ref[...], w_ref[:, c * D_BRANCH:(c + 1) * D_BRANCH], preferred_element_type=F32)

    def put(col, val):
        u_ref[:, col:col + val.shape[1]] = val.astype(BF16)

    def put_cache(dst, val):
        for bb in range(tm // SEQ):
            for h in range(N_ATT_HEADS):
                dst[bb, h] = val[bb * SEQ:(bb + 1) * SEQ, h * HEAD_W:(h + 1) * HEAD_W]

    put(A_FX, proj(W_FX))
    put(A_FG, _silu(proj(W_FG)))
    put(A_PX, proj(W_PX))
    put(A_PG, _silu(proj(W_PG)))
    q = proj(W_Q)
    k = proj(W_K)
    if prompt:
        put_cache(kc_ref, k)
        put(A_Q, q * Q_SCALE)
        put(A_K, k)
    else:
        cos, sa, sb = cos_ref[...], sa_ref[...], sb_ref[...]
        for h in range(N_ATT_HEADS):
            hc = slice(h * HEAD_W, (h + 1) * HEAD_W)
            put(A_Q + h * HEAD_W, _rope(q[:, hc], cos, sa, sb) * Q_SCALE)
            put(A_K + h * HEAD_W, _rope(k[:, hc], cos, sa, sb))
    v = proj(W_V)
    if prompt:
        put_cache(vc_ref, v)
    put(A_V, v)
    put(A_AG, _silu(proj(W_AG)))
    put(A_GLU, proj(W_CA) * _sigmoid(proj(W_CB)))
    put(A_CG, _silu(proj(W_CG)))


def _inproj(x, mod4, norm_g3, w_in_b, l, prompt, rope_tables=None, caches=None):
    t = x.shape[0]
    tm = TOKEN_TILE
    row = _mod_row(prompt, tm)
    in_specs = [pl.BlockSpec((tm, D_MODEL), lambda i: (i, 0)),
                pl.BlockSpec((None, None, 1, 3 * D_MODEL), lambda i: (l, row(i), 0, 0)),
                pl.BlockSpec((None, 1, D_MODEL), lambda i: (l, 0, 0)),
                _resident((None, D_MODEL, D_IN_PROJ), lambda i: (l, 0, 0))]
    args = [x, mod4, norm_g3, w_in_b]
    out_shape = [jax.ShapeDtypeStruct((t, D_ACT), BF16)]
    out_specs = [pl.BlockSpec((tm, D_ACT), lambda i: (i, 0))]
    aliases = {}
    if prompt:
        cache = jax.ShapeDtypeStruct((BATCH, DEPTH, N_ATT_HEADS, SEQ, HEAD_W), F32)
        cache_spec = pl.BlockSpec((tm // SEQ, None, N_ATT_HEADS, SEQ, HEAD_W), lambda i: (i, l, 0, 0, 0))
        out_shape += [cache, cache]
        out_specs += [cache_spec, cache_spec]
        if caches is not None:
            in_specs += [pl.BlockSpec(memory_space=pl.ANY)] * 2
            args += list(caches)
            aliases = {4: 1, 5: 2}
    else:
        nblk = DEC_SEQ // tm
        in_specs += [pl.BlockSpec((tm, HEAD_W), lambda i: (i % nblk, 0))] * 3
        args += list(rope_tables)
    return pl.pallas_call(
        functools.partial(_inproj_kernel, prompt=prompt, aliased_caches=caches is not None),
        out_shape=out_shape,
        grid=(t // tm,),
        in_specs=in_specs,
        out_specs=out_specs,
        scratch_shapes=[pltpu.VMEM((tm, D_MODEL), BF16)],
        input_output_aliases=aliases,
        compiler_params=pltpu.CompilerParams(dimension_semantics=("arbitrary",), vmem_limit_bytes=VMEM_LIMIT),
        name="inproj_ctx" if prompt else "inproj_lat",
    )(*args)


def _outproj_kernel(z_ref, x_ref, mod_ref, w_ref, fg_ref, o_ref, *, final):
    tm = x_ref.shape[0]
    nchunk = D_MODEL // D_BRANCH
    ssq = jnp.zeros((tm, 1), F32)
    for c in range(nchunk):
        cols = slice(c * D_BRANCH, (c + 1) * D_BRANCH)
        y = jnp.dot(z_ref[...], w_ref[:, cols], preferred_element_type=F32)
        gate = mod_ref[:, 2 * D_MODEL + c * D_BRANCH:2 * D_MODEL + (c + 1) * D_BRANCH]
        out = x_ref[:, cols] + gate * y
        o_ref[:, cols] = out
        if final:
            ssq = ssq + jnp.sum(out * out, axis=-1, keepdims=True)
    if final:
        inv = lax.rsqrt(ssq * (1.0 / D_MODEL) + EPS)
        for c in range(nchunk):
            cols = slice(c * D_BRANCH, (c + 1) * D_BRANCH)
            o_ref[:, cols] = o_ref[:, cols] * inv * fg_ref[:, cols]


def _outproj(z, x, mod4, w_out_b, final_g2, l, prompt, final):
    t = x.shape[0]
    tm = TOKEN_TILE
    row = _mod_row(prompt, tm)
    return pl.pallas_call(
        functools.partial(_outproj_kernel, final=final),
        out_shape=jax.ShapeDtypeStruct((t, D_MODEL), F32),
        grid=(t // tm,),
        in_specs=[pl.BlockSpec((tm, D_MODEL), lambda i: (i, 0)),
                  pl.BlockSpec((tm, D_MODEL), lambda i: (i, 0)),
                  pl.BlockSpec((None, None, 1, 3 * D_MODEL), lambda i: (l, row(i), 0, 0)),
                  _resident((None, D_MODEL, D_MODEL), lambda i: (l, 0, 0)),
                  pl.BlockSpec((1, D_MODEL), lambda i: (0, 0))],
        out_specs=pl.BlockSpec((tm, D_MODEL), lambda i: (i, 0)),
        compiler_params=pltpu.CompilerParams(dimension_semantics=("arbitrary",), vmem_limit_bytes=VMEM_LIMIT),
        name="outproj_ctx" if prompt else "outproj_lat",
    )(z, x, mod4, w_out_b, final_g2)


def _make_mixer_kernel(s, l, with_past):
    lam_init = 0.8 - 0.6 * math.exp(-0.3 * l)
    past = PAST_LEN if with_past else 0
    sk = s + past
    nrb = s // ROW_BLOCK
    bq = ROW_BLOCK

    def kern(*refs):
        it = iter(refs)
        u_ref, cs_ref, wf_ref, tp_ref, wp_ref, ps_ref = (next(it) for _ in range(6))
        dl_ref, sg_ref, dw_ref, dwb_ref, lng_ref, lnb_ref, wpw_ref = (next(it) for _ in range(7))
        if with_past:
            ck_ref, cv_ref = next(it), next(it)
        z_ref = next(it)
        ppad_ref, gpad_ref, gs_ref, ybuf_ref, kall_ref, vx_ref = (next(it) for _ in range(6))

        def blk(rb):
            return slice(rb * ROW_BLOCK, (rb + 1) * ROW_BLOCK)

        def cols(c0, g=0, w=D_BRANCH):
            return slice(c0 + g * w, c0 + (g + 1) * w)

        for rb in range(nrb):
            fx = u_ref[:, cols(A_FX)]
            p = jnp.dot(cs_ref[blk(rb), :], fx, preferred_element_type=F32)
            q = jnp.dot(cs_ref[s + rb * ROW_BLOCK:s + (rb + 1) * ROW_BLOCK, :], fx, preferred_element_type=F32)
            yf = (jnp.dot(p.astype(BF16), wf_ref[0:D_BRANCH, :], preferred_element_type=F32)
                  + jnp.dot(q.astype(BF16), wf_ref[D_BRANCH:2 * D_BRANCH, :], preferred_element_type=F32))
            z_ref[blk(rb), cols(0)] = (yf * u_ref[blk(rb), cols(A_FG)].astype(F32)).astype(BF16)

        zero_halo = jnp.zeros((POOL_HALO, D_BRANCH), BF16)
        ppad_ref[0:POOL_HALO, :] = zero_halo
        ppad_ref[POOL_HALO + s:2 * POOL_HALO + s, :] = zero_halo
        ppad_ref[POOL_HALO:POOL_HALO + s, :] = u_ref[:, cols(A_PX)]
        for rb in range(nrb):
            t = rb * ROW_BLOCK + lax.broadcasted_iota(jnp.int32, (ROW_BLOCK, POOL_GROUP), 0)
            for g, w in enumerate(POOL_WINDOWS):
                win = ppad_ref[rb * ROW_BLOCK:(rb + 2) * ROW_BLOCK, cols(0, g, POOL_GROUP)]
                ssum = jnp.dot(tp_ref[g], win, preferred_element_type=F32)
                cnt = (jnp.minimum(t + w // 2, s) - jnp.maximum(t - w // 2, 0)).astype(F32)
                px = u_ref[blk(rb), cols(A_PX, g, POOL_GROUP)].astype(F32)
                pooled = ssum / cnt - px
                pm = jnp.dot(pooled.astype(BF16), wp_ref[g], preferred_element_type=F32)
                pg = u_ref[blk(rb), cols(A_PG, g, POOL_GROUP)].astype(F32)
                z_ref[blk(rb), cols(D_BRANCH, g, POOL_GROUP)] = (
                    pm * ps_ref[:, cols(0, g, POOL_GROUP)] * pg).astype(BF16)

        dl = dl_ref[...]
        lam = (jnp.exp(jnp.sum(dl[0:1] * dl[1:2], axis=-1, keepdims=True))
               - jnp.exp(jnp.sum(dl[2:3] * dl[3:4], axis=-1, keepdims=True)) + lam_init)
        first_map = lax.broadcasted_iota(jnp.int32, (bq, HEAD_W), 1) < DIFF_HEAD
        ones = jnp.ones((sk, HEAD_W), BF16)
        for h in range(N_ATT_HEADS):
            if with_past:
                kall_ref[h, 0:past, :] = ck_ref[h].astype(BF16)
                vx_ref[h, 0:past, 0:HEAD_W] = cv_ref[h].astype(BF16)
                kall_ref[h, past:sk, :] = u_ref[:, cols(A_K, h, HEAD_W)]
            vx_ref[h, past:sk, 0:HEAD_W] = u_ref[:, cols(A_V, h, HEAD_W)]
            vx_ref[h, :, HEAD_W:2 * HEAD_W] = ones
            for qb in range(s // bq):
                qf = u_ref[blk(qb), cols(A_Q, h, HEAD_W)].astype(F32)
                qq = jnp.concatenate([jnp.where(first_map, qf, 0.0).astype(BF16),
                                      jnp.where(first_map, 0.0, qf).astype(BF16)], axis=0)
                keys = kall_ref[h] if with_past else u_ref[:, cols(A_K, h, HEAD_W)]
                sc = lax.dot_general(qq, keys, (((1,), (1,)), ((), ())), preferred_element_type=F32)
                m = jnp.max(sc, axis=-1, keepdims=True)
                p = jnp.exp2((sc - m).astype(BF16))
                ov = jnp.dot(p, vx_ref[h], preferred_element_type=F32)
                o = (ov[0:bq, 0:HEAD_W] / ov[0:bq, HEAD_W:2 * HEAD_W]
                     - lam * (ov[bq:2 * bq, 0:HEAD_W] / ov[bq:2 * bq, HEAD_W:2 * HEAD_W]))
                o = o * lax.rsqrt(jnp.mean(o * o, axis=-1, keepdims=True) + EPS) * sg_ref[...] * (1.0 - lam_init)
                ag = u_ref[blk(qb), cols(A_AG, h, HEAD_W)].astype(F32)
                z_ref[blk(qb), cols(2 * D_BRANCH, h, HEAD_W)] = (o * ag).astype(BF16)

        zero_pad = jnp.zeros((CONV_PAD, D_BRANCH), F32)
        gpad_ref[0:CONV_PAD, :] = zero_pad
        gpad_ref[CONV_PAD + s:2 * CONV_PAD + s, :] = zero_pad
        gpad_ref[CONV_PAD:CONV_PAD + s, :] = u_ref[:, cols(A_GLU)].astype(F32)
        tap_base = CONV_PAD - CONV_WIDTH // 2
        for rb in range(nrb):
            for b in range(1, 8):
                gs_ref[b - 1] = gpad_ref[pl.ds(rb * ROW_BLOCK + b, CONV_WIN), :]

            def conv_rows(i, carry, rb=rb):
                r0 = pl.multiple_of(i * CONV_ROWS, CONV_ROWS)
                acc = jnp.zeros((CONV_ROWS, D_BRANCH), F32)
                for j in range(CONV_WIDTH):
                    a8, b = divmod(j + tap_base, 8)
                    if b == 0:
                        src = gpad_ref[pl.ds(rb * ROW_BLOCK + r0 + 8 * a8, CONV_ROWS), :]
                    else:
                        src = gs_ref[b - 1, pl.ds(r0 + 8 * a8, CONV_ROWS), :]
                    acc = acc + dw_ref[j:j + 1, :] * src
                ybuf_ref[pl.ds(r0, CONV_ROWS), :] = acc + dwb_ref[...]
                return carry

            lax.fori_loop(0, ROW_BLOCK // CONV_ROWS, conv_rows, 0)
            y = ybuf_ref[...]
            mu = jnp.mean(y, axis=-1, keepdims=True)
            d = y - mu
            var = jnp.mean(d * d, axis=-1, keepdims=True)
            act = _silu(d * lax.rsqrt(var + EPS) * lng_ref[...] + lnb_ref[...]).astype(BF16)
            yc = jnp.dot(act, wpw_ref[...], preferred_element_type=F32)
            z_ref[blk(rb), cols(3 * D_BRANCH)] = (yc * u_ref[blk(rb), cols(A_CG)].astype(F32)).astype(BF16)

    return kern


def _mixer(u, consts, params, l, s, cache_k=None, cache_v=None):
    with_past = cache_k is not None
    nseq = u.shape[0] // s
    sk = s + (PAST_LEN if with_past else 0)
    wfold, w_pool_b, pool_scale3, diff_lambda, subln3, conv_dw, dwb3, lng3, lnb3, w_pw_b = params

    def per_layer(shape):
        nd = len(shape)
        return pl.BlockSpec((None,) + shape, lambda b: (l,) + (0,) * nd)

    in_specs = [pl.BlockSpec((s, D_ACT), lambda b: (b, 0)),
                _resident((2 * s, s), lambda b: (0, 0)),
                _resident((None, 2 * D_BRANCH, D_BRANCH), lambda b: (l, 0, 0)),
                _resident((len(POOL_WINDOWS), ROW_BLOCK, 2 * ROW_BLOCK), lambda b: (0, 0, 0)),
                per_layer((len(POOL_WINDOWS), POOL_GROUP, POOL_GROUP)),
                per_layer((1, D_BRANCH)),
                per_layer((4, DIFF_HEAD)),
                per_layer((1, HEAD_W)),
                per_layer((CONV_WIDTH, D_BRANCH)),
                per_layer((1, D_BRANCH)),
                per_layer((1, D_BRANCH)),
                per_layer((1, D_BRANCH)),
                _resident((None, D_BRANCH, D_BRANCH), lambda b: (l, 0, 0))]
    args = [u, consts["cs"], wfold, consts["tpool"], w_pool_b, pool_scale3, diff_lambda, subln3, conv_dw, dwb3,
            lng3, lnb3, w_pw_b]
    if with_past:
        cache_spec = pl.BlockSpec((None, None, N_ATT_HEADS, PAST_LEN, HEAD_W), lambda b: (b, l, 0, 0, 0))
        in_specs += [cache_spec, cache_spec]
        args += [cache_k, cache_v]
    scratch = [pltpu.VMEM((s + 2 * POOL_HALO, D_BRANCH), BF16),
               pltpu.VMEM((s + 2 * CONV_PAD, D_BRANCH), F32),
               pltpu.VMEM((7, CONV_WIN, D_BRANCH), F32),
               pltpu.VMEM((ROW_BLOCK, D_BRANCH), F32),
               pltpu.VMEM((N_ATT_HEADS, sk, HEAD_W), BF16),
               pltpu.VMEM((N_ATT_HEADS, sk, 2 * HEAD_W), BF16)]
    return pl.pallas_call(
        _make_mixer_kernel(s, l, with_past),
        out_shape=jax.ShapeDtypeStruct((nseq * s, D_MODEL), BF16),
        grid=(nseq,),
        in_specs=in_specs,
        out_specs=pl.BlockSpec((s, D_MODEL), lambda b: (b, 0)),
        scratch_shapes=scratch,
        compiler_params=pltpu.CompilerParams(dimension_semantics=("arbitrary",), vmem_limit_bytes=VMEM_LIMIT),
        name="mixer_lat" if with_past else "mixer_ctx",
    )(*args)


def kernel(x_prompt, x_sample, cache_k, cache_v, c, c_ctx, norm_g, w_mod, b_mod, w_in, w_fourier, w_pool,
           pool_scale, diff_lambda, subln_g, conv_dw, conv_dw_b, conv_ln_g, conv_ln_b, w_conv_pw, w_out, final_g):
    w_in_b = w_in.astype(BF16)
    w_out_b = w_out.astype(BF16)
    w_pw_b = w_conv_pw.astype(BF16)
    w_pool_b = w_pool.astype(BF16)
    wfold = _fold_fourier(w_fourier)

    cv = jnp.concatenate([c_ctx[None, :], c, jnp.zeros((MOD_ROWS - 1 - DEC_BATCH, D_MODEL), F32)], axis=0)
    mod4 = _modulation(cv, w_mod, b_mod).reshape(DEPTH, MOD_ROWS, 1, 3 * D_MODEL)

    norm_g3 = norm_g.reshape(DEPTH, 1, D_MODEL)
    final_g2 = final_g.reshape(1, D_MODEL)
    params = (wfold, w_pool_b, pool_scale.reshape(DEPTH, 1, D_BRANCH), diff_lambda,
              subln_g.reshape(DEPTH, 1, HEAD_W), conv_dw, conv_dw_b.reshape(DEPTH, 1, D_BRANCH),
              conv_ln_g.reshape(DEPTH, 1, D_BRANCH), conv_ln_b.reshape(DEPTH, 1, D_BRANCH), w_pw_b)

    tpool = jnp.asarray(_pool_windows()).astype(BF16)
    consts_ctx = {"cs": jnp.asarray(_dft_positions(SEQ)).astype(BF16), "tpool": tpool}
    consts_lat = {"cs": jnp.asarray(_dft_positions(DEC_SEQ)).astype(BF16), "tpool": tpool}
    rope_tables = tuple(jnp.asarray(t) for t in _rope_tables(DEC_SEQ))

    xp = x_prompt.reshape(BATCH * SEQ, D_MODEL)
    xs = x_sample.reshape(DEC_BATCH * DEC_SEQ, D_MODEL)
    caches = None
    for l in range(DEPTH):
        final = l == DEPTH - 1
        up, kc, vc = _inproj(xp, mod4, norm_g3, w_in_b, l, prompt=True, caches=caches)
        caches = (kc, vc)
        zp = _mixer(up, consts_ctx, params, l, SEQ)
        xp = _outproj(zp, xp, mod4, w_out_b, final_g2, l, prompt=True, final=final)

        (us,) = _inproj(xs, mod4, norm_g3, w_in_b, l, prompt=False, rope_tables=rope_tables)
        zs = _mixer(us, consts_lat, params, l, DEC_SEQ, cache_k=cache_k, cache_v=cache_v)
        xs = _outproj(zs, xs, mod4, w_out_b, final_g2, l, prompt=False, final=final)

    return (xp.reshape(BATCH, SEQ, D_MODEL), xs.reshape(DEC_BATCH, DEC_SEQ, D_MODEL), caches[0], caches[1])
```

```python
import functools
import math

import numpy as np
import jax
import jax.numpy as jnp
from jax import lax
from jax.experimental import pallas as pl
from jax.experimental.pallas import tpu as pltpu

D_MODEL = 2048
BATCH = 32
SEQ = 256
DEPTH = 2
DEC_BATCH = 8
DEC_SEQ = 1024
PAST_LEN = 256
GRID_W = 64
D_BRANCH = 512
FFT_GROUP = 128
POOL_WINDOWS = (2, 4, 8, 16)
POOL_GROUP = 128
N_ATT_HEADS = 4
DIFF_HEAD = 64
HEAD_W = 128
N_FREQ = 16
CONV_WIDTH = 31
ROPE_BASE = 10000.0
EPS = 1e-6
N_IN_CHUNKS = 11
D_IN_PROJ = N_IN_CHUNKS * D_BRANCH
MOD_ROWS = 16

W_FX, W_FG, W_PX, W_PG, W_Q, W_K, W_V, W_AG, W_CA, W_CB, W_CG = range(N_IN_CHUNKS)
D_ACT = 5 * D_BRANCH
S_FX, S_PX, S_K, S_V, S_GLU = (i * D_BRANCH for i in range(5))
R_FG, R_PG, R_Q, R_AG, R_CG = (i * D_BRANCH for i in range(5))

Q_SCALE = DIFF_HEAD ** -0.5 * math.log2(math.e)

ROW_BLOCK = 256
POOL_HALO = 128
CONV_PAD = 16
CONV_ROWS = 16
CONV_WIN = ROW_BLOCK + 24
TOKEN_TILE = 512
VMEM_LIMIT = 60 * 1024 * 1024

F32 = jnp.float32
BF16 = jnp.bfloat16


def _sigmoid(x):
    return 0.5 * jnp.tanh(0.5 * x) + 0.5


def _silu(x):
    return x * _sigmoid(x)


def _resident(block_shape, index_map):
    return pl.BlockSpec(block_shape, index_map, pipeline_mode=pl.Buffered(1))


@functools.lru_cache(maxsize=None)
def _dft_positions(s):
    k = np.arange(s, dtype=np.int64)
    ang = 2.0 * np.pi * ((k[:, None] * k[None, :]) % s) / s
    return (np.concatenate([np.cos(ang), np.sin(ang)], axis=0) / np.sqrt(s)).astype(np.float32)


@functools.lru_cache(maxsize=None)
def _dft_channels():
    k = np.arange(FFT_GROUP, dtype=np.int64)
    ang = 2.0 * np.pi * ((k[:, None] * k[None, :]) % FFT_GROUP) / FFT_GROUP
    return (np.stack([np.cos(ang), -np.sin(ang)]) / np.sqrt(FFT_GROUP)).astype(np.float32)


@functools.lru_cache(maxsize=None)
def _pool_windows():
    i = np.arange(ROW_BLOCK)[:, None]
    j = np.arange(2 * ROW_BLOCK)[None, :] - POOL_HALO
    return np.stack([((j >= i - w // 2) & (j < i + w // 2)) for w in POOL_WINDOWS]).astype(np.float32)


@functools.lru_cache(maxsize=None)
def _rope_tables(s):
    rows = s // GRID_W
    t_row = np.repeat(np.arange(rows), GRID_W).astype(np.float64)
    t_col = np.tile(np.arange(GRID_W), rows).astype(np.float64)
    inv = ROPE_BASE ** (-np.arange(N_FREQ, dtype=np.float64) / N_FREQ)
    lane = np.arange(HEAD_W)
    d = lane % DIFF_HEAD
    axis = d // (2 * N_FREQ)
    half = (d % (2 * N_FREQ)) // N_FREQ
    freq = d % N_FREQ
    pos = np.where(axis[None, :] == 0, t_row[:, None], t_col[:, None])
    ang = pos * inv[freq][None, :]
    cos, sin = np.cos(ang), np.sin(ang)
    sin_a = np.where(half[None, :] == 0, -sin, 0.0)
    sin_b = np.where(half[None, :] == 1, sin, 0.0)
    return cos.astype(np.float32), sin_a.astype(np.float32), sin_b.astype(np.float32)


def _fold_kernel(dft_ref, wf_ref, o_ref):
    for part in range(2):
        for g in range(D_BRANCH // FFT_GROUP):
            rows = slice(g * FFT_GROUP, (g + 1) * FFT_GROUP)
            prod = jnp.dot(dft_ref[part], wf_ref[rows, :], preferred_element_type=F32,
                           precision=lax.Precision.HIGHEST)
            o_ref[part * D_BRANCH + g * FFT_GROUP:part * D_BRANCH + (g + 1) * FFT_GROUP, :] = prod.astype(BF16)


def _fold_fourier(w_fourier):
    dft = jnp.asarray(_dft_channels())
    return pl.pallas_call(
        _fold_kernel,
        out_shape=jax.ShapeDtypeStruct((DEPTH, 2 * D_BRANCH, D_BRANCH), BF16),
        grid=(DEPTH,),
        in_specs=[pl.BlockSpec((2, FFT_GROUP, FFT_GROUP), lambda l: (0, 0, 0)),
                  pl.BlockSpec((None, D_BRANCH, D_BRANCH), lambda l: (l, 0, 0))],
        out_specs=pl.BlockSpec((None, 2 * D_BRANCH, D_BRANCH), lambda l: (l, 0, 0)),
        name="fold_fourier",
    )(dft, w_fourier)


def _mod_kernel(cv_ref, w_ref, b_ref, o_ref):
    a = _silu(cv_ref[...]).astype(BF16)
    o_ref[...] = jnp.dot(a, w_ref[...].astype(BF16), preferred_element_type=F32) + b_ref[...]


def _modulation(cv, w_mod, b_mod):
    tn = 1024
    return pl.pallas_call(
        _mod_kernel,
        out_shape=jax.ShapeDtypeStruct((DEPTH, MOD_ROWS, 3 * D_MODEL), F32),
        grid=(DEPTH, 3 * D_MODEL // tn),
        in_specs=[pl.BlockSpec((MOD_ROWS, D_MODEL), lambda l, j: (0, 0)),
                  pl.BlockSpec((None, D_MODEL, tn), lambda l, j: (l, 0, j)),
                  pl.BlockSpec((None, 1, tn), lambda l, j: (l, 0, j))],
        out_specs=pl.BlockSpec((None, MOD_ROWS, tn), lambda l, j: (l, 0, j)),
        compiler_params=pltpu.CompilerParams(vmem_limit_bytes=VMEM_LIMIT),
        name="modulation",
    )(cv, w_mod, b_mod.reshape(DEPTH, 1, 3 * D_MODEL))


def _rope(x, cos, sin_a, sin_b):
    return x * cos + pltpu.roll(x, HEAD_W - N_FREQ, 1) * sin_a + pltpu.roll(x, N_FREQ, 1) * sin_b


def _inproj_kernel(*refs, prompt, aliased_caches):
    it = iter(refs)
    x_ref, mod_ref, g_ref, w_ref = (next(it) for _ in range(4))
    if prompt:
        if aliased_caches:
            next(it), next(it)
        us_ref, ur_ref, kc_ref, vc_ref = (next(it) for _ in range(4))
    else:
        cos_ref, sa_ref, sb_ref, us_ref, ur_ref = (next(it) for _ in range(5))
    h_ref = next(it)

    tm = x_ref.shape[0]
    shift = mod_ref[:, 0:D_MODEL]
    scale = mod_ref[:, D_MODEL:2 * D_MODEL]
    for r in range(tm // 128):
        rows = slice(r * 128, (r + 1) * 128)
        x = x_ref[rows, :]
        y = x * lax.rsqrt(jnp.mean(x * x, axis=-1, keepdims=True) + EPS) * g_ref[...]
        h_ref[rows, :] = (y * (1.0 + scale) + shift).astype(BF16)

    def proj(c):
        return jnp.dot(h_ref[...], w_ref[:, c * D_BRANCH:(c + 1) * D_BRANCH], preferred_element_type=F32)

    def put(dst, col, val):
        dst[:, col:col + val.shape[1]] = val.astype(BF16)

    def put_cache(dst, val):
        for bb in range(tm // SEQ):
            for h in range(N_ATT_HEADS):
                dst[bb, h] = val[bb * SEQ:(bb + 1) * SEQ, h * HEAD_W:(h + 1) * HEAD_W]

    put(us_ref, S_FX, proj(W_FX))
    put(ur_ref, R_FG, _silu(proj(W_FG)))
    put(us_ref, S_PX, proj(W_PX))
    put(ur_ref, R_PG, _silu(proj(W_PG)))
    q = proj(W_Q)
    k = proj(W_K)
    if prompt:
        put_cache(kc_ref, k)
        put(ur_ref, R_Q, q * Q_SCALE)
        put(us_ref, S_K, k)
    else:
        cos, sa, sb = cos_ref[...], sa_ref[...], sb_ref[...]
        for h in range(N_ATT_HEADS):
            hc = slice(h * HEAD_W, (h + 1) * HEAD_W)
            put(ur_ref, R_Q + h * HEAD_W, _rope(q[:, hc], cos, sa, sb) * Q_SCALE)
            put(us_ref, S_K + h * HEAD_W, _rope(k[:, hc], cos, sa, sb))
    v = proj(W_V)
    if prompt:
        put_cache(vc_ref, v)
    put(us_ref, S_V, v)
    put(ur_ref, R_AG, _silu(proj(W_AG)))
    put(us_ref, S_GLU, proj(W_CA) * _sigmoid(proj(W_CB)))
    put(ur_ref, R_CG, _silu(proj(W_CG)))


def _inproj(x, mod4, norm_g3, w_in_b, l, prompt, rope_tables=None, caches=None):
    t = x.shape[0]
    tm = TOKEN_TILE
    if prompt:
        row = lambda i: 0
    else:
        row = lambda i: 1 + (i * tm) // DEC_SEQ
    in_specs = [pl.BlockSpec((tm, D_MODEL), lambda i: (i, 0)),
                pl.BlockSpec((None, None, 1, 3 * D_MODEL), lambda i: (l, row(i), 0, 0)),
                pl.BlockSpec((None, 1, D_MODEL), lambda i: (l, 0, 0)),
                _resident((None, D_MODEL, D_IN_PROJ), lambda i: (l, 0, 0))]
    args = [x, mod4, norm_g3, w_in_b]
    act = jax.ShapeDtypeStruct((t, D_ACT), BF16)
    act_spec = pl.BlockSpec((tm, D_ACT), lambda i: (i, 0))
    out_shape = [act, act]
    out_specs = [act_spec, act_spec]
    aliases = {}
    if prompt:
        cache = jax.ShapeDtypeStruct((BATCH, DEPTH, N_ATT_HEADS, SEQ, HEAD_W), F32)
        cache_spec = pl.BlockSpec((tm // SEQ, None, N_ATT_HEADS, SEQ, HEAD_W), lambda i: (i, l, 0, 0, 0))
        out_shape += [cache, cache]
        out_specs += [cache_spec, cache_spec]
        if caches is not None:
            in_specs += [pl.BlockSpec(memory_space=pl.ANY)] * 2
            args += list(caches)
            aliases = {4: 2, 5: 3}
    else:
        nblk = DEC_SEQ // tm
        in_specs += [pl.BlockSpec((tm, HEAD_W), lambda i: (i % nblk, 0))] * 3
        args += list(rope_tables)
    return pl.pallas_call(
        functools.partial(_inproj_kernel, prompt=prompt, aliased_caches=caches is not None),
        out_shape=out_shape,
        grid=(t // tm,),
        in_specs=in_specs,
        out_specs=out_specs,
        scratch_shapes=[pltpu.VMEM((tm, D_MODEL), BF16)],
        input_output_aliases=aliases,
        compiler_params=pltpu.CompilerParams(dimension_semantics=("arbitrary",), vmem_limit_bytes=VMEM_LIMIT),
        name="inproj_ctx" if prompt else "inproj_lat",
    )(*args)


def _make_mixer_kernel(s, l, with_past, final):
    lam_init = 0.8 - 0.6 * math.exp(-0.3 * l)
    past = PAST_LEN if with_past else 0
    sk = s + past
    nrb = s // ROW_BLOCK
    rb_rows = ROW_BLOCK
    n_out_chunks = D_MODEL // D_BRANCH

    def kern(*refs):
        it = iter(refs)
        us_ref, ur_ref, x_ref, mod_ref, wo_ref, fg_ref = (next(it) for _ in range(6))
        cs_ref, wf_ref, tp_ref, wp_ref, ps_ref = (next(it) for _ in range(5))
        dl_ref, sg_ref, dw_ref, dwb_ref, lng_ref, lnb_ref, wpw_ref = (next(it) for _ in range(7))
        if with_past:
            ck_ref, cv_ref = next(it), next(it)
        o_ref = next(it)
        ppad_ref, gpad_ref, wbuf_ref, gs_ref, ybuf_ref, kall_ref, vx_ref, zcur_ref, zin_ref = (
            next(it) for _ in range(9))

        n = pl.program_id(0)
        n_blocks = pl.num_programs(0) - 1
        cur = jnp.minimum(n, n_blocks - 1)
        if nrb == 1:
            r0 = 0
        else:
            r0 = pl.multiple_of((cur % nrb) * rb_rows, rb_rows)

        def cols(c0, g=0, w=D_BRANCH):
            return slice(c0 + g * w, c0 + (g + 1) * w)

        def rows_at(base, size):
            return pl.ds(base, size) if nrb > 1 else slice(base, base + size)

        def stage_sequence():
            zero_halo = jnp.zeros((POOL_HALO, D_BRANCH), BF16)
            ppad_ref[0:POOL_HALO, :] = zero_halo
            ppad_ref[POOL_HALO + s:2 * POOL_HALO + s, :] = zero_halo
            ppad_ref[POOL_HALO:POOL_HALO + s, :] = us_ref[:, cols(S_PX)]
            zero_pad = jnp.zeros((CONV_PAD, D_BRANCH), F32)
            gpad_ref[0:CONV_PAD, :] = zero_pad
            gpad_ref[CONV_PAD + s:2 * CONV_PAD + s, :] = zero_pad
            gpad_ref[CONV_PAD:CONV_PAD + s, :] = us_ref[:, cols(S_GLU)].astype(F32)
            ones = jnp.ones((sk, HEAD_W), BF16)
            for h in range(N_ATT_HEADS):
                if with_past:
                    kall_ref[h, 0:past, :] = ck_ref[h].astype(BF16)
                    vx_ref[h, 0:past, 0:HEAD_W] = cv_ref[h].astype(BF16)
                kall_ref[h, past:sk, :] = us_ref[:, cols(S_K, h, HEAD_W)]
                vx_ref[h, past:sk, 0:HEAD_W] = us_ref[:, cols(S_V, h, HEAD_W)]
                vx_ref[h, :, HEAD_W:2 * HEAD_W] = ones

        if nrb == 1:
            stage_sequence()
        else:
            pl.when(cur % nrb == 0)(stage_sequence)

        @pl.when(n == 0)
        def _():
            zcur_ref[...] = jnp.zeros(zcur_ref.shape, BF16)

        zin_ref[...] = zcur_ref[...]

        def out_chunk(c):
            y = jnp.dot(zin_ref[...], wo_ref[:, cols(0, c)], preferred_element_type=F32)
            gate = mod_ref[:, 2 * D_MODEL + c * D_BRANCH:2 * D_MODEL + (c + 1) * D_BRANCH]
            out = x_ref[:, cols(0, c)] + gate * y
            o_ref[:, cols(0, c)] = out

        def fourier():
            fx = us_ref[:, cols(S_FX)]
            p = jnp.dot(cs_ref[rows_at(r0, rb_rows), :], fx, preferred_element_type=F32)
            q = jnp.dot(cs_ref[rows_at(s + r0, rb_rows), :], fx, preferred_element_type=F32)
            yf = (jnp.dot(p.astype(BF16), wf_ref[0:D_BRANCH, :], preferred_element_type=F32)
                  + jnp.dot(q.astype(BF16), wf_ref[D_BRANCH:2 * D_BRANCH, :], preferred_element_type=F32))
            zcur_ref[:, cols(0)] = (yf * ur_ref[:, cols(R_FG)].astype(F32)).astype(BF16)

        def pooling():
            t = r0 + lax.broadcasted_iota(jnp.int32, (rb_rows, POOL_GROUP), 0)
            for g, w in enumerate(POOL_WINDOWS):
                win = ppad_ref[rows_at(r0, 2 * rb_rows), cols(0, g, POOL_GROUP)]
                ssum = jnp.dot(tp_ref[g], win, preferred_element_type=F32)
                cnt = (jnp.minimum(t + w // 2, s) - jnp.maximum(t - w // 2, 0)).astype(F32)
                px = ppad_ref[rows_at(r0 + POOL_HALO, rb_rows), cols(0, g, POOL_GROUP)].astype(F32)
                pooled = ssum / cnt - px
                pm = jnp.dot(pooled.astype(BF16), wp_ref[g], preferred_element_type=F32)
                pg = ur_ref[:, cols(R_PG, g, POOL_GROUP)].astype(F32)
                zcur_ref[:, cols(D_BRANCH, g, POOL_GROUP)] = (
                    pm * ps_ref[:, cols(0, g, POOL_GROUP)] * pg).astype(BF16)

        dl = dl_ref[...]
        lam = (jnp.exp(jnp.sum(dl[0:1] * dl[1:2], axis=-1, keepdims=True))
               - jnp.exp(jnp.sum(dl[2:3] * dl[3:4], axis=-1, keepdims=True)) + lam_init)
        first_map = lax.broadcasted_iota(jnp.int32, (rb_rows, HEAD_W), 1) < DIFF_HEAD

        def attention(h):
            qf = ur_ref[:, cols(R_Q, h, HEAD_W)].astype(F32)
            qq = jnp.concatenate([jnp.where(first_map, qf, 0.0).astype(BF16),
                                  jnp.where(first_map, 0.0, qf).astype(BF16)], axis=0)
            sc = lax.dot_general(qq, kall_ref[h], (((1,), (1,)), ((), ())), preferred_element_type=F32)
            m = jnp.max(sc, axis=-1, keepdims=True)
            p = jnp.exp2((sc - m).astype(BF16))
            ov = jnp.dot(p, vx_ref[h], preferred_element_type=F32)
            o = (ov[0:rb_rows, 0:HEAD_W] / ov[0:rb_rows, HEAD_W:2 * HEAD_W]
                 - lam * (ov[rb_rows:2 * rb_rows, 0:HEAD_W] / ov[rb_rows:2 * rb_rows, HEAD_W:2 * HEAD_W]))
            o = o * lax.rsqrt(jnp.mean(o * o, axis=-1, keepdims=True) + EPS) * sg_ref[...] * (1.0 - lam_init)
            ag = ur_ref[:, cols(R_AG, h, HEAD_W)].astype(F32)
            zcur_ref[:, cols(2 * D_BRANCH, h, HEAD_W)] = (o * ag).astype(BF16)

        tap_base = CONV_PAD - CONV_WIDTH // 2

        def conv_stage():
            wbuf_ref[...] = gpad_ref[rows_at(r0, rb_rows + 2 * CONV_PAD), :]
            for b in range(1, 8):
                gs_ref[b - 1] = wbuf_ref[pl.ds(b, CONV_WIN), :]

        def conv_rows(i):
            i0 = i * CONV_ROWS
            acc = jnp.zeros((CONV_ROWS // 8, 8, D_BRANCH), F32)
            for j in range(CONV_WIDTH):
                a8, b = divmod(j + tap_base, 8)
                if b == 0:
                    src = wbuf_ref[i0 + 8 * a8:i0 + 8 * a8 + CONV_ROWS, :]
                else:
                    src = gs_ref[b - 1, i0 + 8 * a8:i0 + 8 * a8 + CONV_ROWS, :]
                acc = acc + dw_ref[j][None] * src.reshape(CONV_ROWS // 8, 8, D_BRANCH)
            ybuf_ref[i0:i0 + CONV_ROWS, :] = acc.reshape(CONV_ROWS, D_BRANCH) + dwb_ref[...]

        def conv_finish():
            y = ybuf_ref[...]
            mu = jnp.mean(y, axis=-1, keepdims=True)
            d = y - mu
            var = jnp.mean(d * d, axis=-1, keepdims=True)
            act = _silu(d * lax.rsqrt(var + EPS) * lng_ref[...] + lnb_ref[...]).astype(BF16)
            yc = jnp.dot(act, wpw_ref[...], preferred_element_type=F32)
            zcur_ref[:, cols(3 * D_BRANCH)] = (yc * ur_ref[:, cols(R_CG)].astype(F32)).astype(BF16)

        n_conv = rb_rows // CONV_ROWS
        per_chunk = n_conv // n_out_chunks
        conv_stage()
        for c in range(n_out_chunks):
            out_chunk(c)
            for i in range(c * per_chunk, (c + 1) * per_chunk):
                conv_rows(i)
        fourier()
        conv_finish()
        pooling()
        for h in range(N_ATT_HEADS):
            attention(h)
        if final:
            ssq = jnp.zeros((rb_rows, 1), F32)
            for c in range(n_out_chunks):
                out = o_ref[:, cols(0, c)]
                ssq = ssq + jnp.sum(out * out, axis=-1, keepdims=True)
            inv = lax.rsqrt(ssq * (1.0 / D_MODEL) + EPS)
            for c in range(n_out_chunks):
                o_ref[:, cols(0, c)] = o_ref[:, cols(0, c)] * inv * fg_ref[:, cols(0, c)]

    return kern


def _mixer(us, ur, x, mod4, w_out_b, final_g2, consts, params, l, s, prompt, final, cache_k=None, cache_v=None):
    with_past = cache_k is not None
    nrb = s // ROW_BLOCK
    n_blocks = x.shape[0] // ROW_BLOCK
    sk = s + (PAST_LEN if with_past else 0)
    wfold, w_pool_b, pool_scale3, diff_lambda, subln3, conv_dw, dwb3, lng3, lnb3, w_pw_b = params

    def cur(n):
        return jnp.minimum(n, n_blocks - 1)

    def prev(n):
        return jnp.maximum(n - 1, 0)

    def mod_row(n):
        return 0 if prompt else 1 + prev(n) // nrb

    def per_layer(shape):
        nd = len(shape)
        return pl.BlockSpec((None,) + shape, lambda n: (l,) + (0,) * nd)

    in_specs = [pl.BlockSpec((s, D_ACT), lambda n: (cur(n) // nrb, 0)),
                pl.BlockSpec((ROW_BLOCK, D_ACT), lambda n: (cur(n), 0)),
                pl.BlockSpec((ROW_BLOCK, D_MODEL), lambda n: (prev(n), 0)),
                pl.BlockSpec((None, None, 1, 3 * D_MODEL), lambda n: (l, mod_row(n), 0, 0)),
                _resident((None, D_MODEL, D_MODEL), lambda n: (l, 0, 0)),
                pl.BlockSpec((1, D_MODEL), lambda n: (0, 0)),
                _resident((2 * s, s), lambda n: (0, 0)),
                _resident((None, 2 * D_BRANCH, D_BRANCH), lambda n: (l, 0, 0)),
                _resident((len(POOL_WINDOWS), ROW_BLOCK, 2 * ROW_BLOCK), lambda n: (0, 0, 0)),
                per_layer((len(POOL_WINDOWS), POOL_GROUP, POOL_GROUP)),
                per_layer((1, D_BRANCH)),
                per_layer((4, DIFF_HEAD)),
                per_layer((1, HEAD_W)),
                per_layer((CONV_WIDTH, 8, D_BRANCH)),
                per_layer((1, D_BRANCH)),
                per_layer((1, D_BRANCH)),
                per_layer((1, D_BRANCH)),
                _resident((None, D_BRANCH, D_BRANCH), lambda n: (l, 0, 0))]
    args = [us, ur, x, mod4, w_out_b, final_g2, consts["cs"], wfold, consts["tpool"], w_pool_b, pool_scale3,
            diff_lambda, subln3, conv_dw, dwb3, lng3, lnb3, w_pw_b]
    if with_past:
        cache_spec = pl.BlockSpec((None, None, N_ATT_HEADS, PAST_LEN, HEAD_W),
                                  lambda n: (cur(n) // nrb, l, 0, 0, 0))
        in_specs += [cache_spec, cache_spec]
        args += [cache_k, cache_v]
    scratch = [pltpu.VMEM((s + 2 * POOL_HALO, D_BRANCH), BF16),
               pltpu.VMEM((s + 2 * CONV_PAD, D_BRANCH), F32),
               pltpu.VMEM((ROW_BLOCK + 2 * CONV_PAD, D_BRANCH), F32),
               pltpu.VMEM((7, CONV_WIN, D_BRANCH), F32),
               pltpu.VMEM((ROW_BLOCK, D_BRANCH), F32),
               pltpu.VMEM((N_ATT_HEADS, sk, HEAD_W), BF16),
               pltpu.VMEM((N_ATT_HEADS, sk, 2 * HEAD_W), BF16),
               pltpu.VMEM((ROW_BLOCK, D_MODEL), BF16),
               pltpu.VMEM((ROW_BLOCK, D_MODEL), BF16)]
    return pl.pallas_call(
        _make_mixer_kernel(s, l, with_past, final),
        out_shape=jax.ShapeDtypeStruct(x.shape, F32),
        grid=(n_blocks + 1,),
        in_specs=in_specs,
        out_specs=pl.BlockSpec((ROW_BLOCK, D_MODEL), lambda n: (prev(n), 0)),
        scratch_shapes=scratch,
        compiler_params=pltpu.CompilerParams(dimension_semantics=("arbitrary",), vmem_limit_bytes=VMEM_LIMIT),
        name="mixer_ctx" if prompt else "mixer_lat",
    )(*args)


def kernel(x_prompt, x_sample, cache_k, cache_v, c, c_ctx, norm_g, w_mod, b_mod, w_in, w_fourier, w_pool,
           pool_scale, diff_lambda, subln_g, conv_dw, conv_dw_b, conv_ln_g, conv_ln_b, w_conv_pw, w_out, final_g):
    w_in_b = w_in.astype(BF16)
    w_out_b = w_out.astype(BF16)
    w_pw_b = w_conv_pw.astype(BF16)
    w_pool_b = w_pool.astype(BF16)
    wfold = _fold_fourier(w_fourier)

    cv = jnp.concatenate([c_ctx[None, :], c, jnp.zeros((MOD_ROWS - 1 - DEC_BATCH, D_MODEL), F32)], axis=0)
    mod4 = _modulation(cv, w_mod, b_mod).reshape(DEPTH, MOD_ROWS, 1, 3 * D_MODEL)

    norm_g3 = norm_g.reshape(DEPTH, 1, D_MODEL)
    final_g2 = final_g.reshape(1, D_MODEL)
    params = (wfold, w_pool_b, pool_scale.reshape(DEPTH, 1, D_BRANCH), diff_lambda,
              subln_g.reshape(DEPTH, 1, HEAD_W),
              jnp.broadcast_to(conv_dw[:, :, None, :], (DEPTH, CONV_WIDTH, 8, D_BRANCH)),
              conv_dw_b.reshape(DEPTH, 1, D_BRANCH),
              conv_ln_g.reshape(DEPTH, 1, D_BRANCH), conv_ln_b.reshape(DEPTH, 1, D_BRANCH), w_pw_b)

    tpool = jnp.asarray(_pool_windows()).astype(BF16)
    consts_ctx = {"cs": jnp.asarray(_dft_positions(SEQ)).astype(BF16), "tpool": tpool}
    consts_lat = {"cs": jnp.asarray(_dft_positions(DEC_SEQ)).astype(BF16), "tpool": tpool}
    rope_tables = tuple(jnp.asarray(t) for t in _rope_tables(DEC_SEQ))

    xp = x_prompt.reshape(BATCH * SEQ, D_MODEL)
    xs = x_sample.reshape(DEC_BATCH * DEC_SEQ, D_MODEL)
    caches = None
    for l in range(DEPTH):
        final = l == DEPTH - 1
        us, ur, kc, vc = _inproj(xp, mod4, norm_g3, w_in_b, l, prompt=True, caches=caches)
        caches = (kc, vc)
        xp = _mixer(us, ur, xp, mod4, w_out_b, final_g2, consts_ctx, params, l, SEQ, prompt=True, final=final)

        us, ur = _inproj(xs, mod4, norm_g3, w_in_b, l, prompt=False, rope_tables=rope_tables)
        xs = _mixer(us, ur, xs, mod4, w_out_b, final_g2, consts_lat, params, l, DEC_SEQ, prompt=False,
                    final=final, cache_k=cache_k, cache_v=cache_v)

    return (xp.reshape(BATCH, SEQ, D_MODEL), xs.reshape(DEC_BATCH, DEC_SEQ, D_MODEL), caches[0], caches[1])
```

```python
import functools
import math

import numpy as np
import jax
import jax.numpy as jnp
from jax import lax
from jax.experimental import pallas as pl
from jax.experimental.pallas import tpu as pltpu

D_MODEL = 2048
BATCH = 32
SEQ = 256
DEPTH = 2
DEC_BATCH = 8
DEC_SEQ = 1024
PAST_LEN = 256
GRID_W = 64
D_BRANCH = 512
FFT_GROUP = 128
POOL_WINDOWS = (2, 4, 8, 16)
POOL_GROUP = 128
N_ATT_HEADS = 4
DIFF_HEAD = 64
HEAD_W = 128
N_FREQ = 16
CONV_WIDTH = 31
ROPE_BASE = 10000.0
EPS = 1e-6
N_IN_CHUNKS = 11
D_IN_PROJ = N_IN_CHUNKS * D_BRANCH
MOD_ROWS = 16

W_FX, W_FG, W_PX, W_PG, W_Q, W_K, W_V, W_AG, W_CA, W_CB, W_CG = range(N_IN_CHUNKS)
D_ACT = 5 * D_BRANCH
S_FX, S_PX, S_K, S_V, S_GLU = (i * D_BRANCH for i in range(5))
R_FG, R_PG, R_Q, R_AG, R_CG = (i * D_BRANCH for i in range(5))

Q_SCALE = DIFF_HEAD ** -0.5 * math.log2(math.e)

ROW_BLOCK = 256
POOL_HALO = 128
CONV_PAD = 16
CONV_DFT = 256
CONV_OUT = 128
CONV_TAIL = CONV_DFT - CONV_OUT - CONV_PAD
CONV_TAPS_PAD = 32
TOKEN_TILE = 512
VMEM_LIMIT = 60 * 1024 * 1024

F32 = jnp.float32
BF16 = jnp.bfloat16


def _sigmoid(x):
    return 0.5 * jnp.tanh(0.5 * x) + 0.5


def _silu(x):
    return x * _sigmoid(x)


def _resident(block_shape, index_map):
    return pl.BlockSpec(block_shape, index_map, pipeline_mode=pl.Buffered(1))


@functools.lru_cache(maxsize=None)
def _dft_positions(s):
    k = np.arange(s, dtype=np.int64)
    ang = 2.0 * np.pi * ((k[:, None] * k[None, :]) % s) / s
    return (np.concatenate([np.cos(ang), np.sin(ang)], axis=0) / np.sqrt(s)).astype(np.float32)


@functools.lru_cache(maxsize=None)
def _dft_channels():
    k = np.arange(FFT_GROUP, dtype=np.int64)
    ang = 2.0 * np.pi * ((k[:, None] * k[None, :]) % FFT_GROUP) / FFT_GROUP
    return (np.stack([np.cos(ang), -np.sin(ang)]) / np.sqrt(FFT_GROUP)).astype(np.float32)


@functools.lru_cache(maxsize=None)
def _pool_windows():
    i = np.arange(ROW_BLOCK)[:, None]
    j = np.arange(2 * ROW_BLOCK)[None, :] - POOL_HALO
    return np.stack([((j >= i - w // 2) & (j < i + w // 2)) for w in POOL_WINDOWS]).astype(np.float32)


@functools.lru_cache(maxsize=None)
def _rope_tables(s):
    rows = s // GRID_W
    t_row = np.repeat(np.arange(rows), GRID_W).astype(np.float64)
    t_col = np.tile(np.arange(GRID_W), rows).astype(np.float64)
    inv = ROPE_BASE ** (-np.arange(N_FREQ, dtype=np.float64) / N_FREQ)
    lane = np.arange(HEAD_W)
    d = lane % DIFF_HEAD
    axis = d // (2 * N_FREQ)
    half = (d % (2 * N_FREQ)) // N_FREQ
    freq = d % N_FREQ
    pos = np.where(axis[None, :] == 0, t_row[:, None], t_col[:, None])
    ang = pos * inv[freq][None, :]
    cos, sin = np.cos(ang), np.sin(ang)
    sin_a = np.where(half[None, :] == 0, -sin, 0.0)
    sin_b = np.where(half[None, :] == 1, sin, 0.0)
    return cos.astype(np.float32), sin_a.astype(np.float32), sin_b.astype(np.float32)


@functools.lru_cache(maxsize=None)
def _conv_dft():
    n, half = CONV_DFT, CONV_DFT // 2
    f = np.arange(half, dtype=np.int64)[:, None]
    u = np.arange(n, dtype=np.int64)[None, :]
    ang = 2.0 * np.pi * ((f * u) % n) / n
    fwd = np.concatenate([np.cos(ang), -np.sin(ang)], axis=0)
    fwd[half, :] = 1.0 - 2.0 * (u[0] % 2)
    t = np.arange(CONV_OUT, dtype=np.int64)[:, None]
    ang = 2.0 * np.pi * ((t * f.T) % n) / n
    inv_re = 2.0 * np.cos(ang) / n
    inv_re[:, 0] = 1.0 / n
    inv_im = -2.0 * np.sin(ang) / n
    inv_im[:, 0] = (1.0 - 2.0 * (t[:, 0] % 2)) / n
    return fwd.astype(np.float32), np.concatenate([inv_re, inv_im], axis=1).astype(np.float32)


@functools.lru_cache(maxsize=None)
def _conv_spectrum_trig():
    n, half = CONV_DFT, CONV_DFT // 2
    f = np.arange(half, dtype=np.int64)[:, None]
    lag = (np.arange(CONV_TAPS_PAD, dtype=np.int64) + CONV_PAD - CONV_WIDTH // 2)[None, :]
    ang = 2.0 * np.pi * ((f * lag) % n) / n
    cos, sin = np.cos(ang), np.sin(ang)
    cos_nyq = cos.copy()
    cos_nyq[0, :] = 1.0 - 2.0 * (lag[0] % 2)
    return np.stack([cos, sin, cos_nyq]).astype(np.float32)


def _spectrum_kernel(trig_ref, dw_ref, o_ref):
    for k in range(3):
        o_ref[k] = jnp.dot(trig_ref[k], dw_ref[...], preferred_element_type=F32, precision=lax.Precision.HIGHEST)


def _conv_spectrum(conv_dw):
    half = CONV_DFT // 2
    dw_pad = jnp.pad(conv_dw, ((0, 0), (0, CONV_TAPS_PAD - CONV_WIDTH), (0, 0)))
    return pl.pallas_call(
        _spectrum_kernel,
        out_shape=jax.ShapeDtypeStruct((DEPTH, 3, half, D_BRANCH), F32),
        grid=(DEPTH,),
        in_specs=[pl.BlockSpec((3, half, CONV_TAPS_PAD), lambda l: (0, 0, 0)),
                  pl.BlockSpec((None, CONV_TAPS_PAD, D_BRANCH), lambda l: (l, 0, 0))],
        out_specs=pl.BlockSpec((None, 3, half, D_BRANCH), lambda l: (l, 0, 0, 0)),
        name="conv_spectrum",
    )(jnp.asarray(_conv_spectrum_trig()), dw_pad)


def _fold_kernel(dft_ref, wf_ref, o_ref):
    for part in range(2):
        for g in range(D_BRANCH // FFT_GROUP):
            rows = slice(g * FFT_GROUP, (g + 1) * FFT_GROUP)
            prod = jnp.dot(dft_ref[part], wf_ref[rows, :], preferred_element_type=F32,
                           precision=lax.Precision.HIGHEST)
            o_ref[part * D_BRANCH + g * FFT_GROUP:part * D_BRANCH + (g + 1) * FFT_GROUP, :] = prod.astype(BF16)


def _fold_fourier(w_fourier):
    dft = jnp.asarray(_dft_channels())
    return pl.pallas_call(
        _fold_kernel,
        out_shape=jax.ShapeDtypeStruct((DEPTH, 2 * D_BRANCH, D_BRANCH), BF16),
        grid=(DEPTH,),
        in_specs=[pl.BlockSpec((2, FFT_GROUP, FFT_GROUP), lambda l: (0, 0, 0)),
                  pl.BlockSpec((None, D_BRANCH, D_BRANCH), lambda l: (l, 0, 0))],
        out_specs=pl.BlockSpec((None, 2 * D_BRANCH, D_BRANCH), lambda l: (l, 0, 0)),
        name="fold_fourier",
    )(dft, w_fourier)


def _mod_kernel(cv_ref, w_ref, b_ref, o_ref):
    a = _silu(cv_ref[...]).astype(BF16)
    o_ref[...] = jnp.dot(a, w_ref[...].astype(BF16), preferred_element_type=F32) + b_ref[...]


def _modulation(cv, w_mod, b_mod):
    tn = 1024
    return pl.pallas_call(
        _mod_kernel,
        out_shape=jax.ShapeDtypeStruct((DEPTH, MOD_ROWS, 3 * D_MODEL), F32),
        grid=(DEPTH, 3 * D_MODEL // tn),
        in_specs=[pl.BlockSpec((MOD_ROWS, D_MODEL), lambda l, j: (0, 0)),
                  pl.BlockSpec((None, D_MODEL, tn), lambda l, j: (l, 0, j)),
                  pl.BlockSpec((None, 1, tn), lambda l, j: (l, 0, j))],
        out_specs=pl.BlockSpec((None, MOD_ROWS, tn), lambda l, j: (l, 0, j)),
        compiler_params=pltpu.CompilerParams(vmem_limit_bytes=VMEM_LIMIT),
        name="modulation",
    )(cv, w_mod, b_mod.reshape(DEPTH, 1, 3 * D_MODEL))


def _rope(x, cos, sin_a, sin_b):
    return x * cos + pltpu.roll(x, HEAD_W - N_FREQ, 1) * sin_a + pltpu.roll(x, N_FREQ, 1) * sin_b


def _inproj_kernel(*refs, prompt, aliased_caches):
    it = iter(refs)
    x_ref, mod_ref, g_ref, w_ref = (next(it) for _ in range(4))
    if prompt:
        if aliased_caches:
            next(it), next(it)
        us_ref, ur_ref, kc_ref, vc_ref = (next(it) for _ in range(4))
    else:
        cos_ref, sa_ref, sb_ref, us_ref, ur_ref = (next(it) for _ in range(5))
    h_ref = next(it)

    tm = x_ref.shape[0]
    shift = mod_ref[:, 0:D_MODEL]
    scale = mod_ref[:, D_MODEL:2 * D_MODEL]
    for r in range(tm // 128):
        rows = slice(r * 128, (r + 1) * 128)
        x = x_ref[rows, :]
        y = x * lax.rsqrt(jnp.mean(x * x, axis=-1, keepdims=True) + EPS) * g_ref[...]
        h_ref[rows, :] = (y * (1.0 + scale) + shift).astype(BF16)

    def proj(c):
        return jnp.dot(h_ref[...], w_ref[:, c * D_BRANCH:(c + 1) * D_BRANCH], preferred_element_type=F32)

    def put(dst, col, val):
        dst[:, col:col + val.shape[1]] = val.astype(BF16)

    def put_cache(dst, val):
        for bb in range(tm // SEQ):
            for h in range(N_ATT_HEADS):
                dst[bb, h] = val[bb * SEQ:(bb + 1) * SEQ, h * HEAD_W:(h + 1) * HEAD_W]

    put(us_ref, S_FX, proj(W_FX))
    put(ur_ref, R_FG, _silu(proj(W_FG)))
    put(us_ref, S_PX, proj(W_PX))
    put(ur_ref, R_PG, _silu(proj(W_PG)))
    q = proj(W_Q)
    k = proj(W_K)
    if prompt:
        put_cache(kc_ref, k)
        put(ur_ref, R_Q, q * Q_SCALE)
        put(us_ref, S_K, k)
    else:
        cos, sa, sb = cos_ref[...], sa_ref[...], sb_ref[...]
        for h in range(N_ATT_HEADS):
            hc = slice(h * HEAD_W, (h + 1) * HEAD_W)
            put(ur_ref, R_Q + h * HEAD_W, _rope(q[:, hc], cos, sa, sb) * Q_SCALE)
            put(us_ref, S_K + h * HEAD_W, _rope(k[:, hc], cos, sa, sb))
    v = proj(W_V)
    if prompt:
        put_cache(vc_ref, v)
    put(us_ref, S_V, v)
    put(ur_ref, R_AG, _silu(proj(W_AG)))
    put(us_ref, S_GLU, proj(W_CA) * _sigmoid(proj(W_CB)))
    put(ur_ref, R_CG, _silu(proj(W_CG)))


def _inproj(x, mod4, norm_g3, w_in_b, l, prompt, rope_tables=None, caches=None):
    t = x.shape[0]
    tm = TOKEN_TILE
    if prompt:
        row = lambda i: 0
    else:
        row = lambda i: 1 + (i * tm) // DEC_SEQ
    in_specs = [pl.BlockSpec((tm, D_MODEL), lambda i: (i, 0)),
                pl.BlockSpec((None, None, 1, 3 * D_MODEL), lambda i: (l, row(i), 0, 0)),
                pl.BlockSpec((None, 1, D_MODEL), lambda i: (l, 0, 0)),
                _resident((None, D_MODEL, D_IN_PROJ), lambda i: (l, 0, 0))]
    args = [x, mod4, norm_g3, w_in_b]
    act = jax.ShapeDtypeStruct((t, D_ACT), BF16)
    act_spec = pl.BlockSpec((tm, D_ACT), lambda i: (i, 0))
    out_shape = [act, act]
    out_specs = [act_spec, act_spec]
    aliases = {}
    if prompt:
        cache = jax.ShapeDtypeStruct((BATCH, DEPTH, N_ATT_HEADS, SEQ, HEAD_W), F32)
        cache_spec = pl.BlockSpec((tm // SEQ, None, N_ATT_HEADS, SEQ, HEAD_W), lambda i: (i, l, 0, 0, 0))
        out_shape += [cache, cache]
        out_specs += [cache_spec, cache_spec]
        if caches is not None:
            in_specs += [pl.BlockSpec(memory_space=pl.ANY)] * 2
            args += list(caches)
            aliases = {4: 2, 5: 3}
    else:
        nblk = DEC_SEQ // tm
        in_specs += [pl.BlockSpec((tm, HEAD_W), lambda i: (i % nblk, 0))] * 3
        args += list(rope_tables)
    return pl.pallas_call(
        functools.partial(_inproj_kernel, prompt=prompt, aliased_caches=caches is not None),
        out_shape=out_shape,
        grid=(t // tm,),
        in_specs=in_specs,
        out_specs=out_specs,
        scratch_shapes=[pltpu.VMEM((tm, D_MODEL), BF16)],
        input_output_aliases=aliases,
        compiler_params=pltpu.CompilerParams(dimension_semantics=("arbitrary",), vmem_limit_bytes=VMEM_LIMIT),
        name="inproj_ctx" if prompt else "inproj_lat",
    )(*args)


def _make_mixer_kernel(s, l, with_past, final):
    lam_init = 0.8 - 0.6 * math.exp(-0.3 * l)
    past = PAST_LEN if with_past else 0
    sk = s + past
    nrb = s // ROW_BLOCK
    rb_rows = ROW_BLOCK
    n_out_chunks = D_MODEL // D_BRANCH

    def kern(*refs):
        it = iter(refs)
        us_ref, ur_ref, x_ref, mod_ref, wo_ref, fg_ref = (next(it) for _ in range(6))
        cs_ref, wf_ref, tp_ref, wp_ref, ps_ref = (next(it) for _ in range(5))
        dl_ref, sg_ref, fwd_ref, inv_ref, spec_ref, dwb_ref, lng_ref, lnb_ref, wpw_ref = (
            next(it) for _ in range(9))
        if with_past:
            ck_ref, cv_ref = next(it), next(it)
        o_ref = next(it)
        ppad_ref, gpad_ref, kall_ref, vx_ref, zcur_ref, zin_ref = (next(it) for _ in range(6))

        n = pl.program_id(0)
        n_blocks = pl.num_programs(0) - 1
        cur = jnp.minimum(n, n_blocks - 1)
        if nrb == 1:
            r0 = 0
        else:
            r0 = pl.multiple_of((cur % nrb) * rb_rows, rb_rows)

        def cols(c0, g=0, w=D_BRANCH):
            return slice(c0 + g * w, c0 + (g + 1) * w)

        def rows_at(base, size):
            return pl.ds(base, size) if nrb > 1 else slice(base, base + size)

        def stage_sequence():
            zero_halo = jnp.zeros((POOL_HALO, D_BRANCH), BF16)
            ppad_ref[0:POOL_HALO, :] = zero_halo
            ppad_ref[POOL_HALO + s:2 * POOL_HALO + s, :] = zero_halo
            ppad_ref[POOL_HALO:POOL_HALO + s, :] = us_ref[:, cols(S_PX)]
            gpad_ref[0:CONV_PAD, :] = jnp.zeros((CONV_PAD, D_BRANCH), BF16)
            gpad_ref[CONV_PAD + s:CONV_PAD + s + CONV_TAIL, :] = jnp.zeros((CONV_TAIL, D_BRANCH), BF16)
            gpad_ref[CONV_PAD:CONV_PAD + s, :] = us_ref[:, cols(S_GLU)]
            ones = jnp.ones((sk, HEAD_W), BF16)
            for h in range(N_ATT_HEADS):
                if with_past:
                    kall_ref[h, 0:past, :] = ck_ref[h].astype(BF16)
                    vx_ref[h, 0:past, 0:HEAD_W] = cv_ref[h].astype(BF16)
                kall_ref[h, past:sk, :] = us_ref[:, cols(S_K, h, HEAD_W)]
                vx_ref[h, past:sk, 0:HEAD_W] = us_ref[:, cols(S_V, h, HEAD_W)]
                vx_ref[h, :, HEAD_W:2 * HEAD_W] = ones

        if nrb == 1:
            stage_sequence()
        else:
            pl.when(cur % nrb == 0)(stage_sequence)

        @pl.when(n == 0)
        def _():
            zcur_ref[...] = jnp.zeros(zcur_ref.shape, BF16)

        zin_ref[...] = zcur_ref[...]

        def out_chunk(c):
            y = jnp.dot(zin_ref[...], wo_ref[:, cols(0, c)], preferred_element_type=F32)
            gate = mod_ref[:, 2 * D_MODEL + c * D_BRANCH:2 * D_MODEL + (c + 1) * D_BRANCH]
            out = x_ref[:, cols(0, c)] + gate * y
            o_ref[:, cols(0, c)] = out

        def fourier():
            fx = us_ref[:, cols(S_FX)]
            p = jnp.dot(cs_ref[rows_at(r0, rb_rows), :], fx, preferred_element_type=F32)
            q = jnp.dot(cs_ref[rows_at(s + r0, rb_rows), :], fx, preferred_element_type=F32)
            yf = (jnp.dot(p.astype(BF16), wf_ref[0:D_BRANCH, :], preferred_element_type=F32)
                  + jnp.dot(q.astype(BF16), wf_ref[D_BRANCH:2 * D_BRANCH, :], preferred_element_type=F32))
            zcur_ref[:, cols(0)] = (yf * ur_ref[:, cols(R_FG)].astype(F32)).astype(BF16)

        def pooling():
            t = r0 + lax.broadcasted_iota(jnp.int32, (rb_rows, POOL_GROUP), 0)
            for g, w in enumerate(POOL_WINDOWS):
                win = ppad_ref[rows_at(r0, 2 * rb_rows), cols(0, g, POOL_GROUP)]
                ssum = jnp.dot(tp_ref[g], win, preferred_element_type=F32)
                cnt = (jnp.minimum(t + w // 2, s) - jnp.maximum(t - w // 2, 0)).astype(F32)
                px = ppad_ref[rows_at(r0 + POOL_HALO, rb_rows), cols(0, g, POOL_GROUP)].astype(F32)
                pooled = ssum / cnt - px
                pm = jnp.dot(pooled.astype(BF16), wp_ref[g], preferred_element_type=F32)
                pg = ur_ref[:, cols(R_PG, g, POOL_GROUP)].astype(F32)
                zcur_ref[:, cols(D_BRANCH, g, POOL_GROUP)] = (
                    pm * ps_ref[:, cols(0, g, POOL_GROUP)] * pg).astype(BF16)

        dl = dl_ref[...]
        lam = (jnp.exp(jnp.sum(dl[0:1] * dl[1:2], axis=-1, keepdims=True))
               - jnp.exp(jnp.sum(dl[2:3] * dl[3:4], axis=-1, keepdims=True)) + lam_init)
        first_map = lax.broadcasted_iota(jnp.int32, (rb_rows, HEAD_W), 1) < DIFF_HEAD

        def attention(h):
            qf = ur_ref[:, cols(R_Q, h, HEAD_W)].astype(F32)
            qq = jnp.concatenate([jnp.where(first_map, qf, 0.0).astype(BF16),
                                  jnp.where(first_map, 0.0, qf).astype(BF16)], axis=0)
            sc = lax.dot_general(qq, kall_ref[h], (((1,), (1,)), ((), ())), preferred_element_type=F32)
            m = jnp.max(sc, axis=-1, keepdims=True)
            p = jnp.exp2((sc - m).astype(BF16))
            ov = jnp.dot(p, vx_ref[h], preferred_element_type=F32)
            o = (ov[0:rb_rows, 0:HEAD_W] / ov[0:rb_rows, HEAD_W:2 * HEAD_W]
                 - lam * (ov[rb_rows:2 * rb_rows, 0:HEAD_W] / ov[rb_rows:2 * rb_rows, HEAD_W:2 * HEAD_W]))
            o = o * lax.rsqrt(jnp.mean(o * o, axis=-1, keepdims=True) + EPS) * sg_ref[...] * (1.0 - lam_init)
            ag = ur_ref[:, cols(R_AG, h, HEAD_W)].astype(F32)
            zcur_ref[:, cols(2 * D_BRANCH, h, HEAD_W)] = (o * ag).astype(BF16)

        def conv_module():
            half = CONV_DFT // 2
            parts = []
            for k in range(rb_rows // CONV_OUT):
                win = gpad_ref[rows_at(r0 + k * CONV_OUT, CONV_DFT), :]
                spec = jnp.dot(fwd_ref[...], win, preferred_element_type=F32)
                yr, yi = spec[0:half], spec[half:CONV_DFT]
                zr = yr * spec_ref[0] - yi * spec_ref[1]
                zi = yr * spec_ref[1] + yi * spec_ref[2]
                z = jnp.concatenate([zr, zi], axis=0).astype(BF16)
                parts.append(jnp.dot(inv_ref[...], z, preferred_element_type=F32))
            y = jnp.concatenate(parts, axis=0) + dwb_ref[...]
            mu = jnp.mean(y, axis=-1, keepdims=True)
            d = y - mu
            var = jnp.mean(d * d, axis=-1, keepdims=True)
            act = _silu(d * lax.rsqrt(var + EPS) * lng_ref[...] + lnb_ref[...]).astype(BF16)
            yc = jnp.dot(act, wpw_ref[...], preferred_element_type=F32)
            zcur_ref[:, cols(3 * D_BRANCH)] = (yc * ur_ref[:, cols(R_CG)].astype(F32)).astype(BF16)

        for c in range(n_out_chunks):
            out_chunk(c)
        conv_module()
        fourier()
        pooling()
        for h in range(N_ATT_HEADS):
            attention(h)
        if final:
            ssq = jnp.zeros((rb_rows, 1), F32)
            for c in range(n_out_chunks):
                out = o_ref[:, cols(0, c)]
                ssq = ssq + jnp.sum(out * out, axis=-1, keepdims=True)
            inv = lax.rsqrt(ssq * (1.0 / D_MODEL) + EPS)
            for c in range(n_out_chunks):
                o_ref[:, cols(0, c)] = o_ref[:, cols(0, c)] * inv * fg_ref[:, cols(0, c)]

    return kern


def _mixer(us, ur, x, mod4, w_out_b, final_g2, consts, params, l, s, prompt, final, cache_k=None, cache_v=None):
    with_past = cache_k is not None
    nrb = s // ROW_BLOCK
    n_blocks = x.shape[0] // ROW_BLOCK
    sk = s + (PAST_LEN if with_past else 0)
    wfold, w_pool_b, pool_scale3, diff_lambda, subln3, conv_spec, dwb3, lng3, lnb3, w_pw_b = params

    def cur(n):
        return jnp.minimum(n, n_blocks - 1)

    def prev(n):
        return jnp.maximum(n - 1, 0)

    def mod_row(n):
        return 0 if prompt else 1 + prev(n) // nrb

    def per_layer(shape):
        nd = len(shape)
        return pl.BlockSpec((None,) + shape, lambda n: (l,) + (0,) * nd)

    in_specs = [pl.BlockSpec((s, D_ACT), lambda n: (cur(n) // nrb, 0)),
                pl.BlockSpec((ROW_BLOCK, D_ACT), lambda n: (cur(n), 0)),
                pl.BlockSpec((ROW_BLOCK, D_MODEL), lambda n: (prev(n), 0)),
                pl.BlockSpec((None, None, 1, 3 * D_MODEL), lambda n: (l, mod_row(n), 0, 0)),
                _resident((None, D_MODEL, D_MODEL), lambda n: (l, 0, 0)),
                pl.BlockSpec((1, D_MODEL), lambda n: (0, 0)),
                _resident((2 * s, s), lambda n: (0, 0)),
                _resident((None, 2 * D_BRANCH, D_BRANCH), lambda n: (l, 0, 0)),
                _resident((len(POOL_WINDOWS), ROW_BLOCK, 2 * ROW_BLOCK), lambda n: (0, 0, 0)),
                per_layer((len(POOL_WINDOWS), POOL_GROUP, POOL_GROUP)),
                per_layer((1, D_BRANCH)),
                per_layer((4, DIFF_HEAD)),
                per_layer((1, HEAD_W)),
                _resident((CONV_DFT, CONV_DFT), lambda n: (0, 0)),
                _resident((CONV_OUT, CONV_DFT), lambda n: (0, 0)),
                _resident((None, 3, CONV_DFT // 2, D_BRANCH), lambda n: (l, 0, 0, 0)),
                per_layer((1, D_BRANCH)),
                per_layer((1, D_BRANCH)),
                per_layer((1, D_BRANCH)),
                _resident((None, D_BRANCH, D_BRANCH), lambda n: (l, 0, 0))]
    args = [us, ur, x, mod4, w_out_b, final_g2, consts["cs"], wfold, consts["tpool"], w_pool_b, pool_scale3,
            diff_lambda, subln3, consts["conv_fwd"], consts["conv_inv"], conv_spec, dwb3, lng3, lnb3, w_pw_b]
    if with_past:
        cache_spec = pl.BlockSpec((None, None, N_ATT_HEADS, PAST_LEN, HEAD_W),
                                  lambda n: (cur(n) // nrb, l, 0, 0, 0))
        in_specs += [cache_spec, cache_spec]
        args += [cache_k, cache_v]
    scratch = [pltpu.VMEM((s + 2 * POOL_HALO, D_BRANCH), BF16),
               pltpu.VMEM((CONV_PAD + s + CONV_TAIL, D_BRANCH), BF16),
               pltpu.VMEM((N_ATT_HEADS, sk, HEAD_W), BF16),
               pltpu.VMEM((N_ATT_HEADS, sk, 2 * HEAD_W), BF16),
               pltpu.VMEM((ROW_BLOCK, D_MODEL), BF16),
               pltpu.VMEM((ROW_BLOCK, D_MODEL), BF16)]
    return pl.pallas_call(
        _make_mixer_kernel(s, l, with_past, final),
        out_shape=jax.ShapeDtypeStruct(x.shape, F32),
        grid=(n_blocks + 1,),
        in_specs=in_specs,
        out_specs=pl.BlockSpec((ROW_BLOCK, D_MODEL), lambda n: (prev(n), 0)),
        scratch_shapes=scratch,
        compiler_params=pltpu.CompilerParams(dimension_semantics=("arbitrary",), vmem_limit_bytes=VMEM_LIMIT),
        name="mixer_ctx" if prompt else "mixer_lat",
    )(*args)


def kernel(x_prompt, x_sample, cache_k, cache_v, c, c_ctx, norm_g, w_mod, b_mod, w_in, w_fourier, w_pool,
           pool_scale, diff_lambda, subln_g, conv_dw, conv_dw_b, conv_ln_g, conv_ln_b, w_conv_pw, w_out, final_g):
    w_in_b = w_in.astype(BF16)
    w_out_b = w_out.astype(BF16)
    w_pw_b = w_conv_pw.astype(BF16)
    w_pool_b = w_pool.astype(BF16)
    wfold = _fold_fourier(w_fourier)

    cv = jnp.concatenate([c_ctx[None, :], c, jnp.zeros((MOD_ROWS - 1 - DEC_BATCH, D_MODEL), F32)], axis=0)
    mod4 = _modulation(cv, w_mod, b_mod).reshape(DEPTH, MOD_ROWS, 1, 3 * D_MODEL)

    norm_g3 = norm_g.reshape(DEPTH, 1, D_MODEL)
    final_g2 = final_g.reshape(1, D_MODEL)
    params = (wfold, w_pool_b, pool_scale.reshape(DEPTH, 1, D_BRANCH), diff_lambda,
              subln_g.reshape(DEPTH, 1, HEAD_W),
              _conv_spectrum(conv_dw),
              conv_dw_b.reshape(DEPTH, 1, D_BRANCH),
              conv_ln_g.reshape(DEPTH, 1, D_BRANCH), conv_ln_b.reshape(DEPTH, 1, D_BRANCH), w_pw_b)

    tpool = jnp.asarray(_pool_windows()).astype(BF16)
    conv_fwd, conv_inv = (jnp.asarray(m).astype(BF16) for m in _conv_dft())
    shared = {"tpool": tpool, "conv_fwd": conv_fwd, "conv_inv": conv_inv}
    consts_ctx = dict(shared, cs=jnp.asarray(_dft_positions(SEQ)).astype(BF16))
    consts_lat = dict(shared, cs=jnp.asarray(_dft_positions(DEC_SEQ)).astype(BF16))
    rope_tables = tuple(jnp.asarray(t) for t in _rope_tables(DEC_SEQ))

    xp = x_prompt.reshape(BATCH * SEQ, D_MODEL)
    xs = x_sample.reshape(DEC_BATCH * DEC_SEQ, D_MODEL)
    caches = None
    for l in range(DEPTH):
        final = l == DEPTH - 1
        us, ur, kc, vc = _inproj(xp, mod4, norm_g3, w_in_b, l, prompt=True, caches=caches)
        caches = (kc, vc)
        xp = _mixer(us, ur, xp, mod4, w_out_b, final_g2, consts_ctx, params, l, SEQ, prompt=True, final=final)

        us, ur = _inproj(xs, mod4, norm_g3, w_in_b, l, prompt=False, rope_tables=rope_tables)
        xs = _mixer(us, ur, xs, mod4, w_out_b, final_g2, consts_lat, params, l, DEC_SEQ, prompt=False,
                    final=final, cache_k=cache_k, cache_v=cache_v)

    return (xp.reshape(BATCH, SEQ, D_MODEL), xs.reshape(DEC_BATCH, DEC_SEQ, D_MODEL), caches[0], caches[1])
```

```python
import functools
import math

import numpy as np
import jax
import jax.numpy as jnp
from jax import lax
from jax.experimental import pallas as pl
from jax.experimental.pallas import tpu as pltpu

D_MODEL = 2048
BATCH = 32
SEQ = 256
DEPTH = 2
DEC_BATCH = 8
DEC_SEQ = 1024
PAST_LEN = 256
GRID_W = 64
D_BRANCH = 512
FFT_GROUP = 128
POOL_WINDOWS = (2, 4, 8, 16)
POOL_GROUP = 128
N_ATT_HEADS = 4
DIFF_HEAD = 64
HEAD_W = 128
N_FREQ = 16
CONV_WIDTH = 31
ROPE_BASE = 10000.0
EPS = 1e-6
N_IN_CHUNKS = 11
D_IN_PROJ = N_IN_CHUNKS * D_BRANCH
MOD_ROWS = 16

W_FX, W_FG, W_PX, W_PG, W_Q, W_K, W_V, W_AG, W_CA, W_CB, W_CG = range(N_IN_CHUNKS)
D_ACT = 5 * D_BRANCH
S_FX, S_PX, S_K, S_V, S_GLU = (i * D_BRANCH for i in range(5))
R_FG, R_PG, R_Q, R_AG, R_CG = (i * D_BRANCH for i in range(5))

Q_SCALE = DIFF_HEAD ** -0.5 * math.log2(math.e)

ROW_BLOCK = 256
POOL_HALO = 128
CONV_PAD = 16
CONV_DFT = 256
CONV_OUT = 128
CONV_TAIL = CONV_DFT - CONV_OUT - CONV_PAD
CONV_TAPS_PAD = 32
TOKEN_TILE = 512
VMEM_LIMIT = 60 * 1024 * 1024

F32 = jnp.float32
BF16 = jnp.bfloat16


def _sigmoid(x):
    return 0.5 * jnp.tanh(0.5 * x) + 0.5


def _silu(x):
    return x * _sigmoid(x)


def _resident(block_shape, index_map):
    return pl.BlockSpec(block_shape, index_map, pipeline_mode=pl.Buffered(1))


@functools.lru_cache(maxsize=None)
def _dft_positions(s):
    k = np.arange(s, dtype=np.int64)
    ang = 2.0 * np.pi * ((k[:, None] * k[None, :]) % s) / s
    return (np.concatenate([np.cos(ang), np.sin(ang)], axis=0) / np.sqrt(s)).astype(np.float32)


@functools.lru_cache(maxsize=None)
def _dft_channels():
    k = np.arange(FFT_GROUP, dtype=np.int64)
    ang = 2.0 * np.pi * ((k[:, None] * k[None, :]) % FFT_GROUP) / FFT_GROUP
    return (np.stack([np.cos(ang), -np.sin(ang)]) / np.sqrt(FFT_GROUP)).astype(np.float32)


@functools.lru_cache(maxsize=None)
def _pool_windows():
    i = np.arange(ROW_BLOCK)[:, None]
    j = np.arange(2 * ROW_BLOCK)[None, :] - POOL_HALO
    return np.stack([((j >= i - w // 2) & (j < i + w // 2)) for w in POOL_WINDOWS]).astype(np.float32)


@functools.lru_cache(maxsize=None)
def _rope_tables(s):
    rows = s // GRID_W
    t_row = np.repeat(np.arange(rows), GRID_W).astype(np.float64)
    t_col = np.tile(np.arange(GRID_W), rows).astype(np.float64)
    inv = ROPE_BASE ** (-np.arange(N_FREQ, dtype=np.float64) / N_FREQ)
    lane = np.arange(HEAD_W)
    d = lane % DIFF_HEAD
    axis = d // (2 * N_FREQ)
    half = (d % (2 * N_FREQ)) // N_FREQ
    freq = d % N_FREQ
    pos = np.where(axis[None, :] == 0, t_row[:, None], t_col[:, None])
    ang = pos * inv[freq][None, :]
    cos, sin = np.cos(ang), np.sin(ang)
    sin_a = np.where(half[None, :] == 0, -sin, 0.0)
    sin_b = np.where(half[None, :] == 1, sin, 0.0)
    return cos.astype(np.float32), sin_a.astype(np.float32), sin_b.astype(np.float32)


@functools.lru_cache(maxsize=None)
def _conv_dft():
    n, half = CONV_DFT, CONV_DFT // 2
    f = np.arange(half, dtype=np.int64)[:, None]
    u = np.arange(n, dtype=np.int64)[None, :]
    ang = 2.0 * np.pi * ((f * u) % n) / n
    fwd = np.concatenate([np.cos(ang), -np.sin(ang)], axis=0)
    fwd[half, :] = 1.0 - 2.0 * (u[0] % 2)
    t = np.arange(CONV_OUT, dtype=np.int64)[:, None]
    ang = 2.0 * np.pi * ((t * f.T) % n) / n
    inv_re = 2.0 * np.cos(ang) / n
    inv_re[:, 0] = 1.0 / n
    inv_im = -2.0 * np.sin(ang) / n
    inv_im[:, 0] = (1.0 - 2.0 * (t[:, 0] % 2)) / n
    return fwd.astype(np.float32), np.concatenate([inv_re, inv_im], axis=1).astype(np.float32)


@functools.lru_cache(maxsize=None)
def _conv_spectrum_trig():
    n, half = CONV_DFT, CONV_DFT // 2
    f = np.arange(half, dtype=np.int64)[:, None]
    lag = (np.arange(CONV_TAPS_PAD, dtype=np.int64) + CONV_PAD - CONV_WIDTH // 2)[None, :]
    ang = 2.0 * np.pi * ((f * lag) % n) / n
    cos, sin = np.cos(ang), np.sin(ang)
    cos_nyq = cos.copy()
    cos_nyq[0, :] = 1.0 - 2.0 * (lag[0] % 2)
    return np.stack([cos, sin, cos_nyq]).astype(np.float32)


def _spectrum_kernel(trig_ref, dw_ref, o_ref):
    for k in range(3):
        o_ref[k] = jnp.dot(trig_ref[k], dw_ref[...], preferred_element_type=F32, precision=lax.Precision.HIGHEST)


def _conv_spectrum(conv_dw):
    half = CONV_DFT // 2
    dw_pad = jnp.pad(conv_dw, ((0, 0), (0, CONV_TAPS_PAD - CONV_WIDTH), (0, 0)))
    return pl.pallas_call(
        _spectrum_kernel,
        out_shape=jax.ShapeDtypeStruct((DEPTH, 3, half, D_BRANCH), F32),
        grid=(DEPTH,),
        in_specs=[pl.BlockSpec((3, half, CONV_TAPS_PAD), lambda l: (0, 0, 0)),
                  pl.BlockSpec((None, CONV_TAPS_PAD, D_BRANCH), lambda l: (l, 0, 0))],
        out_specs=pl.BlockSpec((None, 3, half, D_BRANCH), lambda l: (l, 0, 0, 0)),
        name="conv_spectrum",
    )(jnp.asarray(_conv_spectrum_trig()), dw_pad)


def _fold_kernel(dft_ref, wf_ref, o_ref):
    for part in range(2):
        for g in range(D_BRANCH // FFT_GROUP):
            rows = slice(g * FFT_GROUP, (g + 1) * FFT_GROUP)
            prod = jnp.dot(dft_ref[part], wf_ref[rows, :], preferred_element_type=F32,
                           precision=lax.Precision.HIGHEST)
            o_ref[part * D_BRANCH + g * FFT_GROUP:part * D_BRANCH + (g + 1) * FFT_GROUP, :] = prod.astype(BF16)


def _fold_fourier(w_fourier):
    dft = jnp.asarray(_dft_channels())
    return pl.pallas_call(
        _fold_kernel,
        out_shape=jax.ShapeDtypeStruct((DEPTH, 2 * D_BRANCH, D_BRANCH), BF16),
        grid=(DEPTH,),
        in_specs=[pl.BlockSpec((2, FFT_GROUP, FFT_GROUP), lambda l: (0, 0, 0)),
                  pl.BlockSpec((None, D_BRANCH, D_BRANCH), lambda l: (l, 0, 0))],
        out_specs=pl.BlockSpec((None, 2 * D_BRANCH, D_BRANCH), lambda l: (l, 0, 0)),
        name="fold_fourier",
    )(dft, w_fourier)


def _mod_kernel(cv_ref, w_ref, b_ref, o_ref):
    a = _silu(cv_ref[...]).astype(BF16)
    o_ref[...] = jnp.dot(a, w_ref[...].astype(BF16), preferred_element_type=F32) + b_ref[...]


def _modulation(cv, w_mod, b_mod):
    tn = 1024
    return pl.pallas_call(
        _mod_kernel,
        out_shape=jax.ShapeDtypeStruct((DEPTH, MOD_ROWS, 3 * D_MODEL), F32),
        grid=(DEPTH, 3 * D_MODEL // tn),
        in_specs=[pl.BlockSpec((MOD_ROWS, D_MODEL), lambda l, j: (0, 0)),
                  pl.BlockSpec((None, D_MODEL, tn), lambda l, j: (l, 0, j)),
                  pl.BlockSpec((None, 1, tn), lambda l, j: (l, 0, j))],
        out_specs=pl.BlockSpec((None, MOD_ROWS, tn), lambda l, j: (l, 0, j)),
        compiler_params=pltpu.CompilerParams(vmem_limit_bytes=VMEM_LIMIT),
        name="modulation",
    )(cv, w_mod, b_mod.reshape(DEPTH, 1, 3 * D_MODEL))


def _rope(x, cos, sin_a, sin_b):
    return x * cos + pltpu.roll(x, HEAD_W - N_FREQ, 1) * sin_a + pltpu.roll(x, N_FREQ, 1) * sin_b


def _inproj_kernel(*refs, prompt, aliased_caches, n_cast):
    it = iter(refs)
    x_ref, mod_ref, g_ref, w_ref = (next(it) for _ in range(4))
    cast_src = [next(it) for _ in range(n_cast)]
    if prompt:
        if aliased_caches:
            next(it), next(it)
        us_ref, ur_ref, kc_ref, vc_ref = (next(it) for _ in range(4))
    else:
        cos_ref, sa_ref, sb_ref, us_ref, ur_ref = (next(it) for _ in range(5))
    cast_dst = [next(it) for _ in range(n_cast)]
    h_ref = next(it)

    for src, dst in zip(cast_src, cast_dst):
        dst[...] = src[...].astype(BF16)

    tm = x_ref.shape[0]
    shift = mod_ref[:, 0:D_MODEL]
    scale = mod_ref[:, D_MODEL:2 * D_MODEL]
    for r in range(tm // 128):
        rows = slice(r * 128, (r + 1) * 128)
        x = x_ref[rows, :]
        y = x * lax.rsqrt(jnp.mean(x * x, axis=-1, keepdims=True) + EPS) * g_ref[...]
        h_ref[rows, :] = (y * (1.0 + scale) + shift).astype(BF16)

    def proj(c):
        return jnp.dot(h_ref[...], w_ref[:, c * D_BRANCH:(c + 1) * D_BRANCH], preferred_element_type=F32)

    def put(dst, col, val):
        dst[:, col:col + val.shape[1]] = val.astype(BF16)

    def put_cache(dst, val):
        for bb in range(tm // SEQ):
            for h in range(N_ATT_HEADS):
                dst[bb, h] = val[bb * SEQ:(bb + 1) * SEQ, h * HEAD_W:(h + 1) * HEAD_W]

    put(us_ref, S_FX, proj(W_FX))
    put(ur_ref, R_FG, _silu(proj(W_FG)))
    put(us_ref, S_PX, proj(W_PX))
    put(ur_ref, R_PG, _silu(proj(W_PG)))
    q = proj(W_Q)
    k = proj(W_K)
    if prompt:
        put_cache(kc_ref, k)
        put(ur_ref, R_Q, q * Q_SCALE)
        put(us_ref, S_K, k)
    else:
        cos, sa, sb = cos_ref[...], sa_ref[...], sb_ref[...]
        for h in range(N_ATT_HEADS):
            hc = slice(h * HEAD_W, (h + 1) * HEAD_W)
            put(ur_ref, R_Q + h * HEAD_W, _rope(q[:, hc], cos, sa, sb) * Q_SCALE)
            put(us_ref, S_K + h * HEAD_W, _rope(k[:, hc], cos, sa, sb))
    v = proj(W_V)
    if prompt:
        put_cache(vc_ref, v)
    put(us_ref, S_V, v)
    put(ur_ref, R_AG, _silu(proj(W_AG)))
    put(us_ref, S_GLU, proj(W_CA) * _sigmoid(proj(W_CB)))
    put(ur_ref, R_CG, _silu(proj(W_CG)))


def _inproj(x, mod4, norm_g3, w_in_b, l, prompt, rope_tables=None, caches=None, cast_next=()):
    t = x.shape[0]
    tm = TOKEN_TILE
    steps = t // tm
    if prompt:
        row = lambda i: 0
    else:
        row = lambda i: 1 + (i * tm) // DEC_SEQ
    in_specs = [pl.BlockSpec((tm, D_MODEL), lambda i: (i, 0)),
                pl.BlockSpec((None, None, 1, 3 * D_MODEL), lambda i: (l, row(i), 0, 0)),
                pl.BlockSpec((None, 1, D_MODEL), lambda i: (l, 0, 0)),
                _resident((D_MODEL, D_IN_PROJ), lambda i: (0, 0))]
    args = [x, mod4, norm_g3, w_in_b]
    in_specs += [pl.BlockSpec((None, w.shape[1] // steps, w.shape[2]), lambda i: (l + 1, i, 0)) for w in cast_next]
    args += list(cast_next)
    act = jax.ShapeDtypeStruct((t, D_ACT), BF16)
    act_spec = pl.BlockSpec((tm, D_ACT), lambda i: (i, 0))
    out_shape = [act, act]
    out_specs = [act_spec, act_spec]
    aliases = {}
    if prompt:
        cache = jax.ShapeDtypeStruct((BATCH, DEPTH, N_ATT_HEADS, SEQ, HEAD_W), F32)
        cache_spec = pl.BlockSpec((tm // SEQ, None, N_ATT_HEADS, SEQ, HEAD_W), lambda i: (i, l, 0, 0, 0))
        out_shape += [cache, cache]
        out_specs += [cache_spec, cache_spec]
        if caches is not None:
            aliases = {len(args): 2, len(args) + 1: 3}
            in_specs += [pl.BlockSpec(memory_space=pl.ANY)] * 2
            args += list(caches)
    else:
        nblk = DEC_SEQ // tm
        in_specs += [pl.BlockSpec((tm, HEAD_W), lambda i: (i % nblk, 0))] * 3
        args += list(rope_tables)
    out_shape += [jax.ShapeDtypeStruct(w.shape[1:], BF16) for w in cast_next]
    out_specs += [pl.BlockSpec((w.shape[1] // steps, w.shape[2]), lambda i: (i, 0)) for w in cast_next]
    return pl.pallas_call(
        functools.partial(_inproj_kernel, prompt=prompt, aliased_caches=caches is not None,
                          n_cast=len(cast_next)),
        out_shape=out_shape,
        grid=(steps,),
        in_specs=in_specs,
        out_specs=out_specs,
        scratch_shapes=[pltpu.VMEM((tm, D_MODEL), BF16)],
        input_output_aliases=aliases,
        compiler_params=pltpu.CompilerParams(dimension_semantics=("arbitrary",), vmem_limit_bytes=VMEM_LIMIT),
        name="inproj_ctx" if prompt else "inproj_lat",
    )(*args)


def _make_mixer_kernel(s, l, with_past, final):
    lam_init = 0.8 - 0.6 * math.exp(-0.3 * l)
    past = PAST_LEN if with_past else 0
    sk = s + past
    nrb = s // ROW_BLOCK
    rb_rows = ROW_BLOCK
    n_out_chunks = D_MODEL // D_BRANCH

    def kern(*refs):
        it = iter(refs)
        us_ref, ur_ref, x_ref, xl_ref, mod_ref, wo_ref, fg_ref = (next(it) for _ in range(7))
        cs_ref, wf_ref, tp_ref, wp_ref, ps_ref = (next(it) for _ in range(5))
        dl_ref, sg_ref, fwd_ref, inv_ref, spec_ref, dwb_ref, lng_ref, lnb_ref, wpw_ref = (
            next(it) for _ in range(9))
        if with_past:
            ck_ref, cv_ref = next(it), next(it)
        o_ref, ol_ref = next(it), next(it)
        ppad_ref, gpad_ref, kall_ref, vx_ref, zcur_ref, zin_ref = (next(it) for _ in range(6))

        n = pl.program_id(0)
        if nrb == 1:
            r0 = 0
        else:
            r0 = pl.multiple_of((n % nrb) * rb_rows, rb_rows)

        def cols(c0, g=0, w=D_BRANCH):
            return slice(c0 + g * w, c0 + (g + 1) * w)

        def rows_at(base, size):
            return pl.ds(base, size) if nrb > 1 else slice(base, base + size)

        def stage_sequence():
            zero_halo = jnp.zeros((POOL_HALO, D_BRANCH), BF16)
            ppad_ref[0:POOL_HALO, :] = zero_halo
            ppad_ref[POOL_HALO + s:2 * POOL_HALO + s, :] = zero_halo
            ppad_ref[POOL_HALO:POOL_HALO + s, :] = us_ref[:, cols(S_PX)]
            gpad_ref[0:CONV_PAD, :] = jnp.zeros((CONV_PAD, D_BRANCH), BF16)
            gpad_ref[CONV_PAD + s:CONV_PAD + s + CONV_TAIL, :] = jnp.zeros((CONV_TAIL, D_BRANCH), BF16)
            gpad_ref[CONV_PAD:CONV_PAD + s, :] = us_ref[:, cols(S_GLU)]
            ones = jnp.ones((sk, HEAD_W), BF16)
            for h in range(N_ATT_HEADS):
                if with_past:
                    kall_ref[h, 0:past, :] = ck_ref[h].astype(BF16)
                    vx_ref[h, 0:past, 0:HEAD_W] = cv_ref[h].astype(BF16)
                kall_ref[h, past:sk, :] = us_ref[:, cols(S_K, h, HEAD_W)]
                vx_ref[h, past:sk, 0:HEAD_W] = us_ref[:, cols(S_V, h, HEAD_W)]
                vx_ref[h, :, HEAD_W:2 * HEAD_W] = ones

        if nrb == 1:
            stage_sequence()
        else:
            pl.when(n % nrb == 0)(stage_sequence)

        @pl.when(n == 0)
        def _():
            zcur_ref[...] = jnp.zeros(zcur_ref.shape, BF16)

        zin_ref[...] = zcur_ref[...]

        def out_projection(z_src, x_src, dst):
            for c in range(n_out_chunks):
                y = jnp.dot(z_src[...], wo_ref[:, cols(0, c)], preferred_element_type=F32)
                gate = mod_ref[:, 2 * D_MODEL + c * D_BRANCH:2 * D_MODEL + (c + 1) * D_BRANCH]
                dst[:, cols(0, c)] = x_src[:, cols(0, c)] + gate * y
            if final:
                ssq = jnp.zeros((rb_rows, 1), F32)
                for c in range(n_out_chunks):
                    out = dst[:, cols(0, c)]
                    ssq = ssq + jnp.sum(out * out, axis=-1, keepdims=True)
                inv = lax.rsqrt(ssq * (1.0 / D_MODEL) + EPS)
                for c in range(n_out_chunks):
                    dst[:, cols(0, c)] = dst[:, cols(0, c)] * inv * fg_ref[:, cols(0, c)]

        def fourier():
            fx = us_ref[:, cols(S_FX)]
            p = jnp.dot(cs_ref[rows_at(r0, rb_rows), :], fx, preferred_element_type=F32)
            q = jnp.dot(cs_ref[rows_at(s + r0, rb_rows), :], fx, preferred_element_type=F32)
            yf = (jnp.dot(p.astype(BF16), wf_ref[0:D_BRANCH, :], preferred_element_type=F32)
                  + jnp.dot(q.astype(BF16), wf_ref[D_BRANCH:2 * D_BRANCH, :], preferred_element_type=F32))
            zcur_ref[:, cols(0)] = (yf * ur_ref[:, cols(R_FG)].astype(F32)).astype(BF16)

        def pooling():
            t = r0 + lax.broadcasted_iota(jnp.int32, (rb_rows, POOL_GROUP), 0)
            for g, w in enumerate(POOL_WINDOWS):
                win = ppad_ref[rows_at(r0, 2 * rb_rows), cols(0, g, POOL_GROUP)]
                ssum = jnp.dot(tp_ref[g], win, preferred_element_type=F32)
                cnt = (jnp.minimum(t + w // 2, s) - jnp.maximum(t - w // 2, 0)).astype(F32)
                px = ppad_ref[rows_at(r0 + POOL_HALO, rb_rows), cols(0, g, POOL_GROUP)].astype(F32)
                pooled = ssum / cnt - px
                pm = jnp.dot(pooled.astype(BF16), wp_ref[g], preferred_element_type=F32)
                pg = ur_ref[:, cols(R_PG, g, POOL_GROUP)].astype(F32)
                zcur_ref[:, cols(D_BRANCH, g, POOL_GROUP)] = (
                    pm * ps_ref[:, cols(0, g, POOL_GROUP)] * pg).astype(BF16)

        dl = dl_ref[...]
        lam = (jnp.exp(jnp.sum(dl[0:1] * dl[1:2], axis=-1, keepdims=True))
               - jnp.exp(jnp.sum(dl[2:3] * dl[3:4], axis=-1, keepdims=True)) + lam_init)
        first_map = lax.broadcasted_iota(jnp.int32, (rb_rows, HEAD_W), 1) < DIFF_HEAD

        def attention(h):
            qf = ur_ref[:, cols(R_Q, h, HEAD_W)].astype(F32)
            qq = jnp.concatenate([jnp.where(first_map, qf, 0.0).astype(BF16),
                                  jnp.where(first_map, 0.0, qf).astype(BF16)], axis=0)
            sc = lax.dot_general(qq, kall_ref[h], (((1,), (1,)), ((), ())), preferred_element_type=F32)
            m = jnp.max(sc, axis=-1, keepdims=True)
            p = jnp.exp2((sc - m).astype(BF16))
            ov = jnp.dot(p, vx_ref[h], preferred_element_type=F32)
            o = (ov[0:rb_rows, 0:HEAD_W] / ov[0:rb_rows, HEAD_W:2 * HEAD_W]
                 - lam * (ov[rb_rows:2 * rb_rows, 0:HEAD_W] / ov[rb_rows:2 * rb_rows, HEAD_W:2 * HEAD_W]))
            o = o * lax.rsqrt(jnp.mean(o * o, axis=-1, keepdims=True) + EPS) * sg_ref[...] * (1.0 - lam_init)
            ag = ur_ref[:, cols(R_AG, h, HEAD_W)].astype(F32)
            zcur_ref[:, cols(2 * D_BRANCH, h, HEAD_W)] = (o * ag).astype(BF16)

        def conv_module():
            half = CONV_DFT // 2
            parts = []
            for k in range(rb_rows // CONV_OUT):
                win = gpad_ref[rows_at(r0 + k * CONV_OUT, CONV_DFT), :]
                spec = jnp.dot(fwd_ref[...], win, preferred_element_type=F32)
                yr, yi = spec[0:half], spec[half:CONV_DFT]
                zr = yr * spec_ref[0] - yi * spec_ref[1]
                zi = yr * spec_ref[1] + yi * spec_ref[2]
                z = jnp.concatenate([zr, zi], axis=0).astype(BF16)
                parts.append(jnp.dot(inv_ref[...], z, preferred_element_type=F32))
            y = jnp.concatenate(parts, axis=0) + dwb_ref[...]
            mu = jnp.mean(y, axis=-1, keepdims=True)
            d = y - mu
            var = jnp.mean(d * d, axis=-1, keepdims=True)
            act = _silu(d * lax.rsqrt(var + EPS) * lng_ref[...] + lnb_ref[...]).astype(BF16)
            yc = jnp.dot(act, wpw_ref[...], preferred_element_type=F32)
            zcur_ref[:, cols(3 * D_BRANCH)] = (yc * ur_ref[:, cols(R_CG)].astype(F32)).astype(BF16)

        out_projection(zin_ref, x_ref, o_ref)
        conv_module()
        fourier()
        pooling()
        for h in range(N_ATT_HEADS):
            attention(h)

        @pl.when(n == pl.num_programs(0) - 1)
        def _():
            out_projection(zcur_ref, xl_ref, ol_ref)

    return kern


def _mixer(us, ur, x, mod4, w_out_b, final_g2, consts, params, l, s, prompt, final, cache_k=None, cache_v=None):
    with_past = cache_k is not None
    nrb = s // ROW_BLOCK
    n_blocks = x.shape[0] // ROW_BLOCK
    sk = s + (PAST_LEN if with_past else 0)
    wfold, w_pool_b, pool_scale3, diff_lambda, subln3, conv_spec, dwb3, lng3, lnb3, w_pw_b = params

    assert prompt or nrb >= 2

    def prev(n):
        return jnp.maximum(n - 1, 0)

    def mod_row(n):
        return 0 if prompt else 1 + prev(n) // nrb

    def per_layer(shape):
        nd = len(shape)
        return pl.BlockSpec((None,) + shape, lambda n: (l,) + (0,) * nd)

    in_specs = [pl.BlockSpec((s, D_ACT), lambda n: (n // nrb, 0)),
                pl.BlockSpec((ROW_BLOCK, D_ACT), lambda n: (n, 0)),
                pl.BlockSpec((ROW_BLOCK, D_MODEL), lambda n: (prev(n), 0)),
                pl.BlockSpec((ROW_BLOCK, D_MODEL), lambda n: (n_blocks - 1, 0)),
                pl.BlockSpec((None, None, 1, 3 * D_MODEL), lambda n: (l, mod_row(n), 0, 0)),
                _resident((D_MODEL, D_MODEL), lambda n: (0, 0)),
                pl.BlockSpec((1, D_MODEL), lambda n: (0, 0)),
                _resident((2 * s, s), lambda n: (0, 0)),
                _resident((None, 2 * D_BRANCH, D_BRANCH), lambda n: (l, 0, 0)),
                _resident((len(POOL_WINDOWS), ROW_BLOCK, 2 * ROW_BLOCK), lambda n: (0, 0, 0)),
                per_layer((len(POOL_WINDOWS), POOL_GROUP, POOL_GROUP)),
                per_layer((1, D_BRANCH)),
                per_layer((4, DIFF_HEAD)),
                per_layer((1, HEAD_W)),
                _resident((CONV_DFT, CONV_DFT), lambda n: (0, 0)),
                _resident((CONV_OUT, CONV_DFT), lambda n: (0, 0)),
                _resident((None, 3, CONV_DFT // 2, D_BRANCH), lambda n: (l, 0, 0, 0)),
                per_layer((1, D_BRANCH)),
                per_layer((1, D_BRANCH)),
                per_layer((1, D_BRANCH)),
                _resident((None, D_BRANCH, D_BRANCH), lambda n: (l, 0, 0))]
    args = [us, ur, x, x, mod4, w_out_b, final_g2, consts["cs"], wfold, consts["tpool"], w_pool_b, pool_scale3,
            diff_lambda, subln3, consts["conv_fwd"], consts["conv_inv"], conv_spec, dwb3, lng3, lnb3, w_pw_b]
    if with_past:
        cache_spec = pl.BlockSpec((None, None, N_ATT_HEADS, PAST_LEN, HEAD_W),
                                  lambda n: (n // nrb, l, 0, 0, 0))
        in_specs += [cache_spec, cache_spec]
        args += [cache_k, cache_v]
    scratch = [pltpu.VMEM((s + 2 * POOL_HALO, D_BRANCH), BF16),
               pltpu.VMEM((CONV_PAD + s + CONV_TAIL, D_BRANCH), BF16),
               pltpu.VMEM((N_ATT_HEADS, sk, HEAD_W), BF16),
               pltpu.VMEM((N_ATT_HEADS, sk, 2 * HEAD_W), BF16),
               pltpu.VMEM((ROW_BLOCK, D_MODEL), BF16),
               pltpu.VMEM((ROW_BLOCK, D_MODEL), BF16)]
    out, out_last = pl.pallas_call(
        _make_mixer_kernel(s, l, with_past, final),
        out_shape=[jax.ShapeDtypeStruct(x.shape, F32), jax.ShapeDtypeStruct((ROW_BLOCK, D_MODEL), F32)],
        grid=(n_blocks,),
        in_specs=in_specs,
        out_specs=[pl.BlockSpec((ROW_BLOCK, D_MODEL), lambda n: (prev(n), 0)),
                   pl.BlockSpec((ROW_BLOCK, D_MODEL), lambda n: (0, 0))],
        scratch_shapes=scratch,
        compiler_params=pltpu.CompilerParams(dimension_semantics=("arbitrary",), vmem_limit_bytes=VMEM_LIMIT),
        name="mixer_ctx" if prompt else "mixer_lat",
    )(*args)
    return lax.dynamic_update_slice(out, out_last, (x.shape[0] - ROW_BLOCK, 0))


def kernel(x_prompt, x_sample, cache_k, cache_v, c, c_ctx, norm_g, w_mod, b_mod, w_in, w_fourier, w_pool,
           pool_scale, diff_lambda, subln_g, conv_dw, conv_dw_b, conv_ln_g, conv_ln_b, w_conv_pw, w_out, final_g):
    w_in_b = w_in[0].astype(BF16)
    w_out_b = w_out[0].astype(BF16)
    w_pw_b = w_conv_pw.astype(BF16)
    w_pool_b = w_pool.astype(BF16)
    wfold = _fold_fourier(w_fourier)

    cv = jnp.concatenate([c_ctx[None, :], c, jnp.zeros((MOD_ROWS - 1 - DEC_BATCH, D_MODEL), F32)], axis=0)
    mod4 = _modulation(cv, w_mod, b_mod).reshape(DEPTH, MOD_ROWS, 1, 3 * D_MODEL)

    norm_g3 = norm_g.reshape(DEPTH, 1, D_MODEL)
    final_g2 = final_g.reshape(1, D_MODEL)
    params = (wfold, w_pool_b, pool_scale.reshape(DEPTH, 1, D_BRANCH), diff_lambda,
              subln_g.reshape(DEPTH, 1, HEAD_W),
              _conv_spectrum(conv_dw),
              conv_dw_b.reshape(DEPTH, 1, D_BRANCH),
              conv_ln_g.reshape(DEPTH, 1, D_BRANCH), conv_ln_b.reshape(DEPTH, 1, D_BRANCH), w_pw_b)

    tpool = jnp.asarray(_pool_windows()).astype(BF16)
    conv_fwd, conv_inv = (jnp.asarray(m).astype(BF16) for m in _conv_dft())
    shared = {"tpool": tpool, "conv_fwd": conv_fwd, "conv_inv": conv_inv}
    consts_ctx = dict(shared, cs=jnp.asarray(_dft_positions(SEQ)).astype(BF16))
    consts_lat = dict(shared, cs=jnp.asarray(_dft_positions(DEC_SEQ)).astype(BF16))
    rope_tables = tuple(jnp.asarray(t) for t in _rope_tables(DEC_SEQ))

    xp = x_prompt.reshape(BATCH * SEQ, D_MODEL)
    xs = x_sample.reshape(DEC_BATCH * DEC_SEQ, D_MODEL)
    caches = None
    for l in range(DEPTH):
        final = l == DEPTH - 1
        cast_next = () if final else (w_in, w_out)
        us, ur, kc, vc, *next_weights = _inproj(xp, mod4, norm_g3, w_in_b, l, prompt=True, caches=caches,
                                                cast_next=cast_next)
        caches = (kc, vc)
        xp = _mixer(us, ur, xp, mod4, w_out_b, final_g2, consts_ctx, params, l, SEQ, prompt=True, final=final)

        us, ur = _inproj(xs, mod4, norm_g3, w_in_b, l, prompt=False, rope_tables=rope_tables)
        xs = _mixer(us, ur, xs, mod4, w_out_b, final_g2, consts_lat, params, l, DEC_SEQ, prompt=False,
                    final=final, cache_k=cache_k, cache_v=cache_v)
        if next_weights:
            w_in_b, w_out_b = next_weights

    return (xp.reshape(BATCH, SEQ, D_MODEL), xs.reshape(DEC_BATCH, DEC_SEQ, D_MODEL), caches[0], caches[1])
```

```python
import functools
import math

import numpy as np
import jax
import jax.numpy as jnp
from jax import lax
from jax.experimental import pallas as pl
from jax.experimental.pallas import tpu as pltpu

D_MODEL = 2048
BATCH = 32
SEQ = 256
DEPTH = 2
DEC_BATCH = 8
DEC_SEQ = 1024
PAST_LEN = 256
GRID_W = 64
D_BRANCH = 512
FFT_GROUP = 128
POOL_WINDOWS = (2, 4, 8, 16)
POOL_GROUP = 128
N_ATT_HEADS = 4
DIFF_HEAD = 64
HEAD_W = 128
N_FREQ = 16
CONV_WIDTH = 31
ROPE_BASE = 10000.0
EPS = 1e-6
N_IN_CHUNKS = 11
D_IN_PROJ = N_IN_CHUNKS * D_BRANCH
MOD_ROWS = 16

W_FX, W_FG, W_PX, W_PG, W_Q, W_K, W_V, W_AG, W_CA, W_CB, W_CG = range(N_IN_CHUNKS)
D_ACT = 5 * D_BRANCH
S_FX, S_PX, S_K, S_V, S_GLU = (i * D_BRANCH for i in range(5))
R_FG, R_PG, R_Q, R_AG, R_CG = (i * D_BRANCH for i in range(5))

Q_SCALE = DIFF_HEAD ** -0.5 * math.log2(math.e)

ROW_BLOCK = 256
CONV_PAD = 16
CONV_DFT = 256
CONV_OUT = 128
CONV_TAIL = CONV_DFT - CONV_OUT - CONV_PAD
CONV_TAPS_PAD = 32
TOKEN_TILE = 512
VMEM_LIMIT = 62 * 1024 * 1024

F32 = jnp.float32
BF16 = jnp.bfloat16


def _sigmoid(x):
    return 0.5 * jnp.tanh(0.5 * x) + 0.5


def _silu(x):
    return x * _sigmoid(x)


def _resident(block_shape, index_map):
    return pl.BlockSpec(block_shape, index_map, pipeline_mode=pl.Buffered(1))


@functools.lru_cache(maxsize=None)
def _dft_positions(s):
    k = np.arange(s, dtype=np.int64)
    ang = 2.0 * np.pi * ((k[:, None] * k[None, :]) % s) / s
    return (np.concatenate([np.cos(ang), np.sin(ang)], axis=0) / np.sqrt(s)).astype(np.float32)


@functools.lru_cache(maxsize=None)
def _dft_channels():
    k = np.arange(FFT_GROUP, dtype=np.int64)
    ang = 2.0 * np.pi * ((k[:, None] * k[None, :]) % FFT_GROUP) / FFT_GROUP
    return (np.stack([np.cos(ang), -np.sin(ang)]) / np.sqrt(FFT_GROUP)).astype(np.float32)


@functools.lru_cache(maxsize=None)
def _pool_windows():
    i = np.arange(CONV_OUT)[:, None]
    j = np.arange(CONV_DFT)[None, :] - CONV_PAD
    return np.stack([((j >= i - w // 2) & (j < i + w // 2)) for w in POOL_WINDOWS]).astype(np.float32)


@functools.lru_cache(maxsize=None)
def _rope_tables(s):
    rows = s // GRID_W
    t_row = np.repeat(np.arange(rows), GRID_W).astype(np.float64)
    t_col = np.tile(np.arange(GRID_W), rows).astype(np.float64)
    inv = ROPE_BASE ** (-np.arange(N_FREQ, dtype=np.float64) / N_FREQ)
    lane = np.arange(HEAD_W)
    d = lane % DIFF_HEAD
    axis = d // (2 * N_FREQ)
    half = (d % (2 * N_FREQ)) // N_FREQ
    freq = d % N_FREQ
    pos = np.where(axis[None, :] == 0, t_row[:, None], t_col[:, None])
    ang = pos * inv[freq][None, :]
    cos, sin = np.cos(ang), np.sin(ang)
    sin_a = np.where(half[None, :] == 0, -sin, 0.0)
    sin_b = np.where(half[None, :] == 1, sin, 0.0)
    return cos.astype(np.float32), sin_a.astype(np.float32), sin_b.astype(np.float32)


@functools.lru_cache(maxsize=None)
def _conv_dft():
    n, half = CONV_DFT, CONV_DFT // 2
    f = np.arange(half, dtype=np.int64)[:, None]
    u = np.arange(n, dtype=np.int64)[None, :]
    ang = 2.0 * np.pi * ((f * u) % n) / n
    fwd = np.concatenate([np.cos(ang), -np.sin(ang)], axis=0)
    fwd[half, :] = 1.0 - 2.0 * (u[0] % 2)
    t = np.arange(CONV_OUT, dtype=np.int64)[:, None]
    ang = 2.0 * np.pi * ((t * f.T) % n) / n
    inv_re = 2.0 * np.cos(ang) / n
    inv_re[:, 0] = 1.0 / n
    inv_im = -2.0 * np.sin(ang) / n
    inv_im[:, 0] = (1.0 - 2.0 * (t[:, 0] % 2)) / n
    return fwd.astype(np.float32), np.concatenate([inv_re, inv_im], axis=1).astype(np.float32)


@functools.lru_cache(maxsize=None)
def _conv_spectrum_trig():
    n, half = CONV_DFT, CONV_DFT // 2
    f = np.arange(half, dtype=np.int64)[:, None]
    lag = (np.arange(CONV_TAPS_PAD, dtype=np.int64) + CONV_PAD - CONV_WIDTH // 2)[None, :]
    ang = 2.0 * np.pi * ((f * lag) % n) / n
    cos, sin = np.cos(ang), np.sin(ang)
    cos_nyq = cos.copy()
    cos_nyq[0, :] = 1.0 - 2.0 * (lag[0] % 2)
    return np.stack([cos, sin, cos_nyq]).astype(np.float32)


def _spectrum_kernel(trig_ref, dw_ref, o_ref):
    for k in range(3):
        o_ref[k] = jnp.dot(trig_ref[k], dw_ref[...], preferred_element_type=F32, precision=lax.Precision.HIGHEST)


def _conv_spectrum(conv_dw):
    half = CONV_DFT // 2
    dw_pad = jnp.pad(conv_dw, ((0, 0), (0, CONV_TAPS_PAD - CONV_WIDTH), (0, 0)))
    return pl.pallas_call(
        _spectrum_kernel,
        out_shape=jax.ShapeDtypeStruct((DEPTH, 3, half, D_BRANCH), F32),
        grid=(DEPTH,),
        in_specs=[pl.BlockSpec((3, half, CONV_TAPS_PAD), lambda l: (0, 0, 0)),
                  pl.BlockSpec((None, CONV_TAPS_PAD, D_BRANCH), lambda l: (l, 0, 0))],
        out_specs=pl.BlockSpec((None, 3, half, D_BRANCH), lambda l: (l, 0, 0, 0)),
        name="conv_spectrum",
    )(jnp.asarray(_conv_spectrum_trig()), dw_pad)


def _fold_kernel(dft_ref, wf_ref, o_ref):
    for part in range(2):
        for g in range(D_BRANCH // FFT_GROUP):
            rows = slice(g * FFT_GROUP, (g + 1) * FFT_GROUP)
            prod = jnp.dot(dft_ref[part], wf_ref[rows, :], preferred_element_type=F32,
                           precision=lax.Precision.HIGHEST)
            o_ref[part * D_BRANCH + g * FFT_GROUP:part * D_BRANCH + (g + 1) * FFT_GROUP, :] = prod.astype(BF16)


def _fold_fourier(w_fourier):
    dft = jnp.asarray(_dft_channels())
    return pl.pallas_call(
        _fold_kernel,
        out_shape=jax.ShapeDtypeStruct((DEPTH, 2 * D_BRANCH, D_BRANCH), BF16),
        grid=(DEPTH,),
        in_specs=[pl.BlockSpec((2, FFT_GROUP, FFT_GROUP), lambda l: (0, 0, 0)),
                  pl.BlockSpec((None, D_BRANCH, D_BRANCH), lambda l: (l, 0, 0))],
        out_specs=pl.BlockSpec((None, 2 * D_BRANCH, D_BRANCH), lambda l: (l, 0, 0)),
        name="fold_fourier",
    )(dft, w_fourier)


def _fold_pool_kernel(wpx_ref, wp_ref, o_ref):
    for g in range(len(POOL_WINDOWS)):
        gc = slice(g * POOL_GROUP, (g + 1) * POOL_GROUP)
        o_ref[:, gc] = jnp.dot(wpx_ref[:, gc], wp_ref[g], preferred_element_type=F32,
                               precision=lax.Precision.HIGHEST).astype(BF16)


def _fold_pool(w_in, w_pool):
    return pl.pallas_call(
        _fold_pool_kernel,
        out_shape=jax.ShapeDtypeStruct((DEPTH, D_MODEL, D_BRANCH), BF16),
        grid=(DEPTH,),
        in_specs=[pl.BlockSpec((None, D_MODEL, D_BRANCH), lambda l: (l, 0, W_PX)),
                  pl.BlockSpec((None, len(POOL_WINDOWS), POOL_GROUP, POOL_GROUP), lambda l: (l, 0, 0, 0))],
        out_specs=pl.BlockSpec((None, D_MODEL, D_BRANCH), lambda l: (l, 0, 0)),
        name="fold_pool",
    )(w_in, w_pool)


def _mod_kernel(cv_ref, w_ref, b_ref, o_ref):
    a = _silu(cv_ref[...]).astype(BF16)
    o_ref[...] = jnp.dot(a, w_ref[...].astype(BF16), preferred_element_type=F32) + b_ref[...]


def _modulation(cv, w_mod, b_mod):
    tn = 1024
    return pl.pallas_call(
        _mod_kernel,
        out_shape=jax.ShapeDtypeStruct((DEPTH, MOD_ROWS, 3 * D_MODEL), F32),
        grid=(DEPTH, 3 * D_MODEL // tn),
        in_specs=[pl.BlockSpec((MOD_ROWS, D_MODEL), lambda l, j: (0, 0)),
                  pl.BlockSpec((None, D_MODEL, tn), lambda l, j: (l, 0, j)),
                  pl.BlockSpec((None, 1, tn), lambda l, j: (l, 0, j))],
        out_specs=pl.BlockSpec((None, MOD_ROWS, tn), lambda l, j: (l, 0, j)),
        compiler_params=pltpu.CompilerParams(vmem_limit_bytes=VMEM_LIMIT),
        name="modulation",
    )(cv, w_mod, b_mod.reshape(DEPTH, 1, 3 * D_MODEL))


def _rope(x, cos, sin_a, sin_b):
    return x * cos + pltpu.roll(x, HEAD_W - N_FREQ, 1) * sin_a + pltpu.roll(x, N_FREQ, 1) * sin_b


def _inproj_kernel(*refs, prompt, aliased_caches, n_cast):
    it = iter(refs)
    x_ref, mod_ref, g_ref, w_ref, wpx_ref = (next(it) for _ in range(5))
    cast_src = [next(it) for _ in range(n_cast)]
    if prompt:
        if aliased_caches:
            next(it), next(it)
        us_ref, ur_ref, kc_ref, vc_ref = (next(it) for _ in range(4))
    else:
        cos_ref, sa_ref, sb_ref, us_ref, ur_ref = (next(it) for _ in range(5))
    cast_dst = [next(it) for _ in range(n_cast)]
    h_ref = next(it)

    for src, dst in zip(cast_src, cast_dst):
        dst[...] = src[...].astype(BF16)

    tm = x_ref.shape[0]
    shift = mod_ref[:, 0:D_MODEL]
    scale = mod_ref[:, D_MODEL:2 * D_MODEL]
    for r in range(tm // 128):
        rows = slice(r * 128, (r + 1) * 128)
        x = x_ref[rows, :]
        y = x * lax.rsqrt(jnp.mean(x * x, axis=-1, keepdims=True) + EPS) * g_ref[...]
        h_ref[rows, :] = (y * (1.0 + scale) + shift).astype(BF16)

    def proj(c):
        return jnp.dot(h_ref[...], w_ref[:, c * D_BRANCH:(c + 1) * D_BRANCH], preferred_element_type=F32)

    def put(dst, col, val):
        dst[:, col:col + val.shape[1]] = val.astype(BF16)

    def put_cache(dst, val):
        for bb in range(tm // SEQ):
            for h in range(N_ATT_HEADS):
                dst[bb, h] = val[bb * SEQ:(bb + 1) * SEQ, h * HEAD_W:(h + 1) * HEAD_W]

    put(us_ref, S_FX, proj(W_FX))
    put(ur_ref, R_FG, _silu(proj(W_FG)))
    put(us_ref, S_PX, jnp.dot(h_ref[...], wpx_ref[...], preferred_element_type=F32))
    put(ur_ref, R_PG, _silu(proj(W_PG)))
    q = proj(W_Q)
    k = proj(W_K)
    if prompt:
        put_cache(kc_ref, k)
        put(ur_ref, R_Q, q * Q_SCALE)
        put(us_ref, S_K, k)
    else:
        cos, sa, sb = cos_ref[...], sa_ref[...], sb_ref[...]
        for h in range(N_ATT_HEADS):
            hc = slice(h * HEAD_W, (h + 1) * HEAD_W)
            put(ur_ref, R_Q + h * HEAD_W, _rope(q[:, hc], cos, sa, sb) * Q_SCALE)
            put(us_ref, S_K + h * HEAD_W, _rope(k[:, hc], cos, sa, sb))
    v = proj(W_V)
    if prompt:
        put_cache(vc_ref, v)
    put(us_ref, S_V, v)
    put(ur_ref, R_AG, _silu(proj(W_AG)))
    put(us_ref, S_GLU, proj(W_CA) * _sigmoid(proj(W_CB)))
    put(ur_ref, R_CG, _silu(proj(W_CG)))


def _inproj(x, mod4, norm_g3, w_in_b, w_px, l, prompt, rope_tables=None, caches=None, cast_next=()):
    t = x.shape[0]
    tm = TOKEN_TILE
    steps = t // tm
    if prompt:
        row = lambda i: 0
    else:
        row = lambda i: 1 + (i * tm) // DEC_SEQ
    in_specs = [pl.BlockSpec((tm, D_MODEL), lambda i: (i, 0)),
                pl.BlockSpec((None, None, 1, 3 * D_MODEL), lambda i: (l, row(i), 0, 0)),
                pl.BlockSpec((None, 1, D_MODEL), lambda i: (l, 0, 0)),
                _resident((D_MODEL, D_IN_PROJ), lambda i: (0, 0)),
                _resident((None, D_MODEL, D_BRANCH), lambda i: (l, 0, 0))]
    args = [x, mod4, norm_g3, w_in_b, w_px]
    in_specs += [pl.BlockSpec((None, w.shape[1] // steps, w.shape[2]), lambda i: (l + 1, i, 0)) for w in cast_next]
    args += list(cast_next)
    act = jax.ShapeDtypeStruct((t, D_ACT), BF16)
    act_spec = pl.BlockSpec((tm, D_ACT), lambda i: (i, 0))
    out_shape = [act, act]
    out_specs = [act_spec, act_spec]
    aliases = {}
    if prompt:
        cache = jax.ShapeDtypeStruct((BATCH, DEPTH, N_ATT_HEADS, SEQ, HEAD_W), F32)
        cache_spec = pl.BlockSpec((tm // SEQ, None, N_ATT_HEADS, SEQ, HEAD_W), lambda i: (i, l, 0, 0, 0))
        out_shape += [cache, cache]
        out_specs += [cache_spec, cache_spec]
        if caches is not None:
            aliases = {len(args): 2, len(args) + 1: 3}
            in_specs += [pl.BlockSpec(memory_space=pl.ANY)] * 2
            args += list(caches)
    else:
        nblk = DEC_SEQ // tm
        in_specs += [pl.BlockSpec((tm, HEAD_W), lambda i: (i % nblk, 0))] * 3
        args += list(rope_tables)
    out_shape += [jax.ShapeDtypeStruct(w.shape[1:], BF16) for w in cast_next]
    out_specs += [pl.BlockSpec((w.shape[1] // steps, w.shape[2]), lambda i: (i, 0)) for w in cast_next]
    return pl.pallas_call(
        functools.partial(_inproj_kernel, prompt=prompt, aliased_caches=caches is not None,
                          n_cast=len(cast_next)),
        out_shape=out_shape,
        grid=(steps,),
        in_specs=in_specs,
        out_specs=out_specs,
        scratch_shapes=[pltpu.VMEM((tm, D_MODEL), BF16)],
        input_output_aliases=aliases,
        compiler_params=pltpu.CompilerParams(dimension_semantics=("arbitrary",), vmem_limit_bytes=VMEM_LIMIT),
        name="inproj_ctx" if prompt else "inproj_lat",
    )(*args)


def _make_mixer_kernel(s, l, with_past, final):
    lam_init = 0.8 - 0.6 * math.exp(-0.3 * l)
    past = PAST_LEN if with_past else 0
    sk = s + past
    nrb = s // ROW_BLOCK
    rb_rows = ROW_BLOCK
    n_out_chunks = D_MODEL // D_BRANCH

    def kern(*refs):
        it = iter(refs)
        us_ref, ur_ref, x_ref, xl_ref, mod_ref, wo_ref, fg_ref = (next(it) for _ in range(7))
        cs_ref, wf_ref, tp_ref, ps_ref = (next(it) for _ in range(4))
        dl_ref, sg_ref, fwd_ref, inv_ref, spec_ref, dwb_ref, lng_ref, lnb_ref, wpw_ref = (
            next(it) for _ in range(9))
        if with_past:
            ck_ref, cv_ref = next(it), next(it)
        o_ref, ol_ref = next(it), next(it)
        ppad_ref, gpad_ref, kall_ref, vx_ref, zcur_ref, zin_ref = (next(it) for _ in range(6))

        n = pl.program_id(0)
        if nrb == 1:
            r0 = 0
        else:
            r0 = pl.multiple_of((n % nrb) * rb_rows, rb_rows)

        def cols(c0, g=0, w=D_BRANCH):
            return slice(c0 + g * w, c0 + (g + 1) * w)

        def rows_at(base, size):
            return pl.ds(base, size) if nrb > 1 else slice(base, base + size)

        def stage_sequence():
            for pad_ref, col0 in ((ppad_ref, S_PX), (gpad_ref, S_GLU)):
                pad_ref[0:CONV_PAD, :] = jnp.zeros((CONV_PAD, D_BRANCH), BF16)
                pad_ref[CONV_PAD + s:CONV_PAD + s + CONV_TAIL, :] = jnp.zeros((CONV_TAIL, D_BRANCH), BF16)
                pad_ref[CONV_PAD:CONV_PAD + s, :] = us_ref[:, cols(col0)]
            ones = jnp.ones((sk, HEAD_W), BF16)
            for h in range(N_ATT_HEADS):
                if with_past:
                    kall_ref[h, 0:past, :] = ck_ref[h].astype(BF16)
                    vx_ref[h, 0:past, 0:HEAD_W] = cv_ref[h].astype(BF16)
                kall_ref[h, past:sk, :] = us_ref[:, cols(S_K, h, HEAD_W)]
                vx_ref[h, past:sk, 0:HEAD_W] = us_ref[:, cols(S_V, h, HEAD_W)]
                vx_ref[h, :, HEAD_W:2 * HEAD_W] = ones

        if nrb == 1:
            stage_sequence()
        else:
            pl.when(n % nrb == 0)(stage_sequence)

        @pl.when(n == 0)
        def _():
            zcur_ref[...] = jnp.zeros(zcur_ref.shape, BF16)

        zin_ref[...] = zcur_ref[...]

        def out_projection(z_src, x_src, dst):
            for c in range(n_out_chunks):
                y = jnp.dot(z_src[...], wo_ref[:, cols(0, c)], preferred_element_type=F32)
                gate = mod_ref[:, 2 * D_MODEL + c * D_BRANCH:2 * D_MODEL + (c + 1) * D_BRANCH]
                dst[:, cols(0, c)] = x_src[:, cols(0, c)] + gate * y
            if final:
                ssq = jnp.zeros((rb_rows, 1), F32)
                for c in range(n_out_chunks):
                    out = dst[:, cols(0, c)]
                    ssq = ssq + jnp.sum(out * out, axis=-1, keepdims=True)
                inv = lax.rsqrt(ssq * (1.0 / D_MODEL) + EPS)
                for c in range(n_out_chunks):
                    dst[:, cols(0, c)] = dst[:, cols(0, c)] * inv * fg_ref[:, cols(0, c)]

        def fourier():
            fx = us_ref[:, cols(S_FX)]
            p = jnp.dot(cs_ref[rows_at(r0, rb_rows), :], fx, preferred_element_type=F32)
            q = jnp.dot(cs_ref[rows_at(s + r0, rb_rows), :], fx, preferred_element_type=F32)
            yf = (jnp.dot(p.astype(BF16), wf_ref[0:D_BRANCH, :], preferred_element_type=F32)
                  + jnp.dot(q.astype(BF16), wf_ref[D_BRANCH:2 * D_BRANCH, :], preferred_element_type=F32))
            zcur_ref[:, cols(0)] = (yf * ur_ref[:, cols(R_FG)].astype(F32)).astype(BF16)

        def pooling():
            for k in range(rb_rows // CONV_OUT):
                base = r0 + k * CONV_OUT
                t = base + lax.broadcasted_iota(jnp.int32, (CONV_OUT, POOL_GROUP), 0)
                out_rows = slice(k * CONV_OUT, (k + 1) * CONV_OUT)
                for g, w in enumerate(POOL_WINDOWS):
                    win = ppad_ref[rows_at(base, CONV_DFT), cols(0, g, POOL_GROUP)]
                    ssum = jnp.dot(tp_ref[g], win, preferred_element_type=F32)
                    cnt = (jnp.minimum(t + w // 2, s) - jnp.maximum(t - w // 2, 0)).astype(F32)
                    px = ppad_ref[rows_at(base + CONV_PAD, CONV_OUT), cols(0, g, POOL_GROUP)].astype(F32)
                    pg = ur_ref[out_rows, cols(R_PG, g, POOL_GROUP)].astype(F32)
                    zcur_ref[out_rows, cols(D_BRANCH, g, POOL_GROUP)] = (
                        (ssum / cnt - px) * ps_ref[:, cols(0, g, POOL_GROUP)] * pg).astype(BF16)

        dl = dl_ref[...]
        lam = (jnp.exp(jnp.sum(dl[0:1] * dl[1:2], axis=-1, keepdims=True))
               - jnp.exp(jnp.sum(dl[2:3] * dl[3:4], axis=-1, keepdims=True)) + lam_init)
        first_map = lax.broadcasted_iota(jnp.int32, (rb_rows, HEAD_W), 1) < DIFF_HEAD

        def attention(h):
            qf = ur_ref[:, cols(R_Q, h, HEAD_W)].astype(F32)
            qq = jnp.concatenate([jnp.where(first_map, qf, 0.0).astype(BF16),
                                  jnp.where(first_map, 0.0, qf).astype(BF16)], axis=0)
            sc = lax.dot_general(qq, kall_ref[h], (((1,), (1,)), ((), ())), preferred_element_type=F32)
            m = jnp.max(sc, axis=-1, keepdims=True)
            p = jnp.exp2((sc - m).astype(BF16))
            ov = jnp.dot(p, vx_ref[h], preferred_element_type=F32)
            o = (ov[0:rb_rows, 0:HEAD_W] / ov[0:rb_rows, HEAD_W:2 * HEAD_W]
                 - lam * (ov[rb_rows:2 * rb_rows, 0:HEAD_W] / ov[rb_rows:2 * rb_rows, HEAD_W:2 * HEAD_W]))
            o = o * lax.rsqrt(jnp.mean(o * o, axis=-1, keepdims=True) + EPS) * sg_ref[...] * (1.0 - lam_init)
            ag = ur_ref[:, cols(R_AG, h, HEAD_W)].astype(F32)
            zcur_ref[:, cols(2 * D_BRANCH, h, HEAD_W)] = (o * ag).astype(BF16)

        def conv_module():
            half = CONV_DFT // 2
            parts = []
            for k in range(rb_rows // CONV_OUT):
                win = gpad_ref[rows_at(r0 + k * CONV_OUT, CONV_DFT), :]
                spec = jnp.dot(fwd_ref[...], win, preferred_element_type=F32)
                yr, yi = spec[0:half], spec[half:CONV_DFT]
                zr = yr * spec_ref[0] - yi * spec_ref[1]
                zi = yr * spec_ref[1] + yi * spec_ref[2]
                z = jnp.concatenate([zr, zi], axis=0).astype(BF16)
                parts.append(jnp.dot(inv_ref[...], z, preferred_element_type=F32))
            y = jnp.concatenate(parts, axis=0) + dwb_ref[...]
            mu = jnp.mean(y, axis=-1, keepdims=True)
            d = y - mu
            var = jnp.mean(d * d, axis=-1, keepdims=True)
            act = _silu(d * lax.rsqrt(var + EPS) * lng_ref[...] + lnb_ref[...]).astype(BF16)
            yc = jnp.dot(act, wpw_ref[...], preferred_element_type=F32)
            zcur_ref[:, cols(3 * D_BRANCH)] = (yc * ur_ref[:, cols(R_CG)].astype(F32)).astype(BF16)

        out_projection(zin_ref, x_ref, o_ref)
        conv_module()
        fourier()
        pooling()
        for h in range(N_ATT_HEADS):
            attention(h)

        @pl.when(n == pl.num_programs(0) - 1)
        def _():
            out_projection(zcur_ref, xl_ref, ol_ref)

    return kern


def _mixer(us, ur, x, mod4, w_out_b, final_g2, consts, params, l, s, prompt, final, cache_k=None, cache_v=None):
    with_past = cache_k is not None
    nrb = s // ROW_BLOCK
    n_blocks = x.shape[0] // ROW_BLOCK
    sk = s + (PAST_LEN if with_past else 0)
    wfold, pool_scale3, diff_lambda, subln3, conv_spec, dwb3, lng3, lnb3, w_pw_b = params

    assert prompt or nrb >= 2

    def prev(n):
        return jnp.maximum(n - 1, 0)

    def mod_row(n):
        return 0 if prompt else 1 + prev(n) // nrb

    def per_layer(shape):
        nd = len(shape)
        return pl.BlockSpec((None,) + shape, lambda n: (l,) + (0,) * nd)

    in_specs = [pl.BlockSpec((s, D_ACT), lambda n: (n // nrb, 0)),
                pl.BlockSpec((ROW_BLOCK, D_ACT), lambda n: (n, 0)),
                pl.BlockSpec((ROW_BLOCK, D_MODEL), lambda n: (prev(n), 0)),
                pl.BlockSpec((ROW_BLOCK, D_MODEL), lambda n: (n_blocks - 1, 0)),
                pl.BlockSpec((None, None, 1, 3 * D_MODEL), lambda n: (l, mod_row(n), 0, 0)),
                _resident((D_MODEL, D_MODEL), lambda n: (0, 0)),
                pl.BlockSpec((1, D_MODEL), lambda n: (0, 0)),
                _resident((2 * s, s), lambda n: (0, 0)),
                _resident((None, 2 * D_BRANCH, D_BRANCH), lambda n: (l, 0, 0)),
                _resident((len(POOL_WINDOWS), CONV_OUT, CONV_DFT), lambda n: (0, 0, 0)),
                per_layer((1, D_BRANCH)),
                per_layer((4, DIFF_HEAD)),
                per_layer((1, HEAD_W)),
                _resident((CONV_DFT, CONV_DFT), lambda n: (0, 0)),
                _resident((CONV_OUT, CONV_DFT), lambda n: (0, 0)),
                _resident((None, 3, CONV_DFT // 2, D_BRANCH), lambda n: (l, 0, 0, 0)),
                per_layer((1, D_BRANCH)),
                per_layer((1, D_BRANCH)),
                per_layer((1, D_BRANCH)),
                _resident((None, D_BRANCH, D_BRANCH), lambda n: (l, 0, 0))]
    args = [us, ur, x, x, mod4, w_out_b, final_g2, consts["cs"], wfold, consts["tpool"], pool_scale3,
            diff_lambda, subln3, consts["conv_fwd"], consts["conv_inv"], conv_spec, dwb3, lng3, lnb3, w_pw_b]
    if with_past:
        cache_spec = pl.BlockSpec((None, None, N_ATT_HEADS, PAST_LEN, HEAD_W),
                                  lambda n: (n // nrb, l, 0, 0, 0))
        in_specs += [cache_spec, cache_spec]
        args += [cache_k, cache_v]
    scratch = [pltpu.VMEM((CONV_PAD + s + CONV_TAIL, D_BRANCH), BF16),
               pltpu.VMEM((CONV_PAD + s + CONV_TAIL, D_BRANCH), BF16),
               pltpu.VMEM((N_ATT_HEADS, sk, HEAD_W), BF16),
               pltpu.VMEM((N_ATT_HEADS, sk, 2 * HEAD_W), BF16),
               pltpu.VMEM((ROW_BLOCK, D_MODEL), BF16),
               pltpu.VMEM((ROW_BLOCK, D_MODEL), BF16)]
    out, out_last = pl.pallas_call(
        _make_mixer_kernel(s, l, with_past, final),
        out_shape=[jax.ShapeDtypeStruct(x.shape, F32), jax.ShapeDtypeStruct((ROW_BLOCK, D_MODEL), F32)],
        grid=(n_blocks,),
        in_specs=in_specs,
        out_specs=[pl.BlockSpec((ROW_BLOCK, D_MODEL), lambda n: (prev(n), 0)),
                   pl.BlockSpec((ROW_BLOCK, D_MODEL), lambda n: (0, 0))],
        scratch_shapes=scratch,
        compiler_params=pltpu.CompilerParams(dimension_semantics=("arbitrary",), vmem_limit_bytes=VMEM_LIMIT),
        name="mixer_ctx" if prompt else "mixer_lat",
    )(*args)
    return lax.dynamic_update_slice(out, out_last, (x.shape[0] - ROW_BLOCK, 0))


def kernel(x_prompt, x_sample, cache_k, cache_v, c, c_ctx, norm_g, w_mod, b_mod, w_in, w_fourier, w_pool,
           pool_scale, diff_lambda, subln_g, conv_dw, conv_dw_b, conv_ln_g, conv_ln_b, w_conv_pw, w_out, final_g):
    w_in_b = w_in[0].astype(BF16)
    w_out_b = w_out[0].astype(BF16)
    w_pw_b = w_conv_pw.astype(BF16)
    w_px = _fold_pool(w_in, w_pool)
    wfold = _fold_fourier(w_fourier)

    cv = jnp.concatenate([c_ctx[None, :], c, jnp.zeros((MOD_ROWS - 1 - DEC_BATCH, D_MODEL), F32)], axis=0)
    mod4 = _modulation(cv, w_mod, b_mod).reshape(DEPTH, MOD_ROWS, 1, 3 * D_MODEL)

    norm_g3 = norm_g.reshape(DEPTH, 1, D_MODEL)
    final_g2 = final_g.reshape(1, D_MODEL)
    params = (wfold, pool_scale.reshape(DEPTH, 1, D_BRANCH), diff_lambda,
              subln_g.reshape(DEPTH, 1, HEAD_W),
              _conv_spectrum(conv_dw),
              conv_dw_b.reshape(DEPTH, 1, D_BRANCH),
              conv_ln_g.reshape(DEPTH, 1, D_BRANCH), conv_ln_b.reshape(DEPTH, 1, D_BRANCH), w_pw_b)

    tpool = jnp.asarray(_pool_windows()).astype(BF16)
    conv_fwd, conv_inv = (jnp.asarray(m).astype(BF16) for m in _conv_dft())
    shared = {"tpool": tpool, "conv_fwd": conv_fwd, "conv_inv": conv_inv}
    consts_ctx = dict(shared, cs=jnp.asarray(_dft_positions(SEQ)).astype(BF16))
    consts_lat = dict(shared, cs=jnp.asarray(_dft_positions(DEC_SEQ)).astype(BF16))
    rope_tables = tuple(jnp.asarray(t) for t in _rope_tables(DEC_SEQ))

    xp = x_prompt.reshape(BATCH * SEQ, D_MODEL)
    xs = x_sample.reshape(DEC_BATCH * DEC_SEQ, D_MODEL)
    caches = None
    for l in range(DEPTH):
        final = l == DEPTH - 1
        cast_next = () if final else (w_in, w_out)
        us, ur, kc, vc, *next_weights = _inproj(xp, mod4, norm_g3, w_in_b, w_px, l, prompt=True, caches=caches,
                                                cast_next=cast_next)
        caches = (kc, vc)
        xp = _mixer(us, ur, xp, mod4, w_out_b, final_g2, consts_ctx, params, l, SEQ, prompt=True, final=final)

        us, ur = _inproj(xs, mod4, norm_g3, w_in_b, w_px, l, prompt=False, rope_tables=rope_tables)
        xs = _mixer(us, ur, xs, mod4, w_out_b, final_g2, consts_lat, params, l, DEC_SEQ, prompt=False,
                    final=final, cache_k=cache_k, cache_v=cache_v)
        if next_weights:
            w_in_b, w_out_b = next_weights

    return (xp.reshape(BATCH, SEQ, D_MODEL), xs.reshape(DEC_BATCH, DEC_SEQ, D_MODEL), caches[0], caches[1])
```

```python
import functools
import math

import numpy as np
import jax
import jax.numpy as jnp
from jax import lax
from jax.experimental import pallas as pl
from jax.experimental.pallas import tpu as pltpu

D_MODEL = 2048
BATCH = 32
SEQ = 256
DEPTH = 2
DEC_BATCH = 8
DEC_SEQ = 1024
PAST_LEN = 256
GRID_W = 64
D_BRANCH = 512
FFT_GROUP = 128
POOL_WINDOWS = (2, 4, 8, 16)
POOL_GROUP = 128
N_ATT_HEADS = 4
DIFF_HEAD = 64
HEAD_W = 128
N_FREQ = 16
CONV_WIDTH = 31
ROPE_BASE = 10000.0
EPS = 1e-6
N_IN_CHUNKS = 11
D_IN_PROJ = N_IN_CHUNKS * D_BRANCH
MOD_ROWS = 16

W_FX, W_FG, W_PX, W_PG, W_Q, W_K, W_V, W_AG, W_CA, W_CB, W_CG = range(N_IN_CHUNKS)
D_ACT = 5 * D_BRANCH
S_FX, S_PX, S_K, S_V, S_GLU = (i * D_BRANCH for i in range(5))
R_FG, R_PG, R_Q, R_AG, R_CG = (i * D_BRANCH for i in range(5))

Q_SCALE = DIFF_HEAD ** -0.5 * math.log2(math.e)

ROW_BLOCK = 256
CONV_PAD = 16
CONV_DFT = 256
CONV_OUT = 128
CONV_TAIL = CONV_DFT - CONV_OUT - CONV_PAD
CONV_TAPS_PAD = 32
TOKEN_TILE = 512
VMEM_LIMIT = 62 * 1024 * 1024

F32 = jnp.float32
BF16 = jnp.bfloat16


def _sigmoid(x):
    return 0.5 * jnp.tanh(0.5 * x) + 0.5


def _silu(x):
    return x * _sigmoid(x)


def _resident(block_shape, index_map):
    return pl.BlockSpec(block_shape, index_map, pipeline_mode=pl.Buffered(1))


@functools.lru_cache(maxsize=None)
def _dft_positions(s):
    k = np.arange(s, dtype=np.int64)
    ang = 2.0 * np.pi * ((k[:, None] * k[None, :]) % s) / s
    return (np.concatenate([np.cos(ang), np.sin(ang)], axis=0) / np.sqrt(s)).astype(np.float32)


@functools.lru_cache(maxsize=None)
def _dft_channels():
    k = np.arange(FFT_GROUP, dtype=np.int64)
    ang = 2.0 * np.pi * ((k[:, None] * k[None, :]) % FFT_GROUP) / FFT_GROUP
    return (np.stack([np.cos(ang), -np.sin(ang)]) / np.sqrt(FFT_GROUP)).astype(np.float32)


@functools.lru_cache(maxsize=None)
def _pool_windows():
    i = np.arange(CONV_OUT)[:, None]
    j = np.arange(CONV_DFT)[None, :] - CONV_PAD
    return np.stack([((j >= i - w // 2) & (j < i + w // 2)) for w in POOL_WINDOWS]).astype(np.float32)


@functools.lru_cache(maxsize=None)
def _rope_tables(s):
    rows = s // GRID_W
    t_row = np.repeat(np.arange(rows), GRID_W).astype(np.float64)
    t_col = np.tile(np.arange(GRID_W), rows).astype(np.float64)
    inv = ROPE_BASE ** (-np.arange(N_FREQ, dtype=np.float64) / N_FREQ)
    lane = np.arange(HEAD_W)
    d = lane % DIFF_HEAD
    axis = d // (2 * N_FREQ)
    half = (d % (2 * N_FREQ)) // N_FREQ
    freq = d % N_FREQ
    pos = np.where(axis[None, :] == 0, t_row[:, None], t_col[:, None])
    ang = pos * inv[freq][None, :]
    cos, sin = np.cos(ang), np.sin(ang)
    sin_a = np.where(half[None, :] == 0, -sin, 0.0)
    sin_b = np.where(half[None, :] == 1, sin, 0.0)
    return cos.astype(np.float32), sin_a.astype(np.float32), sin_b.astype(np.float32)


@functools.lru_cache(maxsize=None)
def _conv_dft():
    n, half = CONV_DFT, CONV_DFT // 2
    f = np.arange(half, dtype=np.int64)[:, None]
    u = np.arange(n, dtype=np.int64)[None, :]
    ang = 2.0 * np.pi * ((f * u) % n) / n
    fwd = np.concatenate([np.cos(ang), -np.sin(ang)], axis=0)
    fwd[half, :] = 1.0 - 2.0 * (u[0] % 2)
    t = np.arange(CONV_OUT, dtype=np.int64)[:, None]
    ang = 2.0 * np.pi * ((t * f.T) % n) / n
    inv_re = 2.0 * np.cos(ang) / n
    inv_re[:, 0] = 1.0 / n
    inv_im = -2.0 * np.sin(ang) / n
    inv_im[:, 0] = (1.0 - 2.0 * (t[:, 0] % 2)) / n
    return fwd.astype(np.float32), np.concatenate([inv_re, inv_im], axis=1).astype(np.float32)


@functools.lru_cache(maxsize=None)
def _conv_spectrum_trig():
    n, half = CONV_DFT, CONV_DFT // 2
    f = np.arange(half, dtype=np.int64)[:, None]
    lag = (np.arange(CONV_TAPS_PAD, dtype=np.int64) + CONV_PAD - CONV_WIDTH // 2)[None, :]
    ang = 2.0 * np.pi * ((f * lag) % n) / n
    cos, sin = np.cos(ang), np.sin(ang)
    cos_nyq = cos.copy()
    cos_nyq[0, :] = 1.0 - 2.0 * (lag[0] % 2)
    return np.stack([cos, sin, cos_nyq]).astype(np.float32)


def _spectrum_kernel(trig_ref, dw_ref, o_ref):
    for k in range(3):
        o_ref[k] = jnp.dot(trig_ref[k], dw_ref[...], preferred_element_type=F32, precision=lax.Precision.HIGHEST)


def _conv_spectrum(conv_dw):
    half = CONV_DFT // 2
    dw_pad = jnp.pad(conv_dw, ((0, 0), (0, CONV_TAPS_PAD - CONV_WIDTH), (0, 0)))
    return pl.pallas_call(
        _spectrum_kernel,
        out_shape=jax.ShapeDtypeStruct((DEPTH, 3, half, D_BRANCH), F32),
        grid=(DEPTH,),
        in_specs=[pl.BlockSpec((3, half, CONV_TAPS_PAD), lambda l: (0, 0, 0)),
                  pl.BlockSpec((None, CONV_TAPS_PAD, D_BRANCH), lambda l: (l, 0, 0))],
        out_specs=pl.BlockSpec((None, 3, half, D_BRANCH), lambda l: (l, 0, 0, 0)),
        name="conv_spectrum",
    )(jnp.asarray(_conv_spectrum_trig()), dw_pad)


def _fold_kernel(dft_ref, wf_ref, o_ref):
    for part in range(2):
        for g in range(D_BRANCH // FFT_GROUP):
            rows = slice(g * FFT_GROUP, (g + 1) * FFT_GROUP)
            prod = jnp.dot(dft_ref[part], wf_ref[rows, :], preferred_element_type=F32,
                           precision=lax.Precision.HIGHEST)
            o_ref[part * D_BRANCH + g * FFT_GROUP:part * D_BRANCH + (g + 1) * FFT_GROUP, :] = prod.astype(BF16)


def _fold_fourier(w_fourier):
    dft = jnp.asarray(_dft_channels())
    return pl.pallas_call(
        _fold_kernel,
        out_shape=jax.ShapeDtypeStruct((DEPTH, 2 * D_BRANCH, D_BRANCH), BF16),
        grid=(DEPTH,),
        in_specs=[pl.BlockSpec((2, FFT_GROUP, FFT_GROUP), lambda l: (0, 0, 0)),
                  pl.BlockSpec((None, D_BRANCH, D_BRANCH), lambda l: (l, 0, 0))],
        out_specs=pl.BlockSpec((None, 2 * D_BRANCH, D_BRANCH), lambda l: (l, 0, 0)),
        name="fold_fourier",
    )(dft, w_fourier)


def _fold_pool_kernel(wpx_ref, wp_ref, o_ref):
    for g in range(len(POOL_WINDOWS)):
        gc = slice(g * POOL_GROUP, (g + 1) * POOL_GROUP)
        o_ref[:, gc] = jnp.dot(wpx_ref[:, gc].astype(BF16), wp_ref[g].astype(BF16),
                               preferred_element_type=F32).astype(BF16)


def _fold_pool(w_in, w_pool):
    return pl.pallas_call(
        _fold_pool_kernel,
        out_shape=jax.ShapeDtypeStruct((DEPTH, D_MODEL, D_BRANCH), BF16),
        grid=(DEPTH,),
        in_specs=[pl.BlockSpec((None, D_MODEL, D_BRANCH), lambda l: (l, 0, W_PX)),
                  pl.BlockSpec((None, len(POOL_WINDOWS), POOL_GROUP, POOL_GROUP), lambda l: (l, 0, 0, 0))],
        out_specs=pl.BlockSpec((None, D_MODEL, D_BRANCH), lambda l: (l, 0, 0)),
        name="fold_pool",
    )(w_in, w_pool)


def _mod_kernel(cv_ref, w_ref, b_ref, o_ref):
    a = _silu(cv_ref[...]).astype(BF16)
    o_ref[...] = jnp.dot(a, w_ref[...].astype(BF16), preferred_element_type=F32) + b_ref[...]


def _modulation(cv, w_mod, b_mod):
    tn = 1024
    return pl.pallas_call(
        _mod_kernel,
        out_shape=jax.ShapeDtypeStruct((DEPTH, MOD_ROWS, 3 * D_MODEL), F32),
        grid=(DEPTH, 3 * D_MODEL // tn),
        in_specs=[pl.BlockSpec((MOD_ROWS, D_MODEL), lambda l, j: (0, 0)),
                  pl.BlockSpec((None, D_MODEL, tn), lambda l, j: (l, 0, j)),
                  pl.BlockSpec((None, 1, tn), lambda l, j: (l, 0, j))],
        out_specs=pl.BlockSpec((None, MOD_ROWS, tn), lambda l, j: (l, 0, j)),
        compiler_params=pltpu.CompilerParams(vmem_limit_bytes=VMEM_LIMIT),
        name="modulation",
    )(cv, w_mod, b_mod.reshape(DEPTH, 1, 3 * D_MODEL))


def _rope(x, cos, sin_a, sin_b):
    return x * cos + pltpu.roll(x, HEAD_W - N_FREQ, 1) * sin_a + pltpu.roll(x, N_FREQ, 1) * sin_b


def _inproj_kernel(*refs, prompt, aliased_caches, n_cast):
    it = iter(refs)
    x_ref, mod_ref, g_ref, w_ref, wpx_ref = (next(it) for _ in range(5))
    cast_src = [next(it) for _ in range(n_cast)]
    if prompt:
        if aliased_caches:
            next(it), next(it)
        us_ref, ur_ref, kc_ref, vc_ref = (next(it) for _ in range(4))
    else:
        cos_ref, sa_ref, sb_ref, us_ref, ur_ref = (next(it) for _ in range(5))
    cast_dst = [next(it) for _ in range(n_cast)]
    h_ref = next(it)

    for src, dst in zip(cast_src, cast_dst):
        dst[...] = src[...].astype(BF16)

    tm = x_ref.shape[0]
    shift = mod_ref[:, 0:D_MODEL]
    scale = mod_ref[:, D_MODEL:2 * D_MODEL]
    for r in range(tm // 128):
        rows = slice(r * 128, (r + 1) * 128)
        x = x_ref[rows, :]
        y = x * lax.rsqrt(jnp.mean(x * x, axis=-1, keepdims=True) + EPS) * g_ref[...]
        h_ref[rows, :] = (y * (1.0 + scale) + shift).astype(BF16)

    def proj(c):
        return jnp.dot(h_ref[...], w_ref[:, c * D_BRANCH:(c + 1) * D_BRANCH], preferred_element_type=F32)

    def put(dst, col, val):
        dst[:, col:col + val.shape[1]] = val.astype(BF16)

    def put_cache(dst, val):
        for bb in range(tm // SEQ):
            for h in range(N_ATT_HEADS):
                dst[bb, h] = val[bb * SEQ:(bb + 1) * SEQ, h * HEAD_W:(h + 1) * HEAD_W]

    put(us_ref, S_FX, proj(W_FX))
    put(ur_ref, R_FG, _silu(proj(W_FG)))
    put(us_ref, S_PX, jnp.dot(h_ref[...], wpx_ref[...], preferred_element_type=F32))
    put(ur_ref, R_PG, _silu(proj(W_PG)))
    q = proj(W_Q)
    k = proj(W_K)
    if prompt:
        put_cache(kc_ref, k)
        put(ur_ref, R_Q, q * Q_SCALE)
        put(us_ref, S_K, k)
    else:
        cos, sa, sb = cos_ref[...], sa_ref[...], sb_ref[...]
        for h in range(N_ATT_HEADS):
            hc = slice(h * HEAD_W, (h + 1) * HEAD_W)
            put(ur_ref, R_Q + h * HEAD_W, _rope(q[:, hc], cos, sa, sb) * Q_SCALE)
            put(us_ref, S_K + h * HEAD_W, _rope(k[:, hc], cos, sa, sb))
    v = proj(W_V)
    if prompt:
        put_cache(vc_ref, v)
    put(us_ref, S_V, v)
    put(ur_ref, R_AG, _silu(proj(W_AG)))
    put(us_ref, S_GLU, proj(W_CA) * _sigmoid(proj(W_CB)))
    put(ur_ref, R_CG, _silu(proj(W_CG)))


def _inproj(x, mod4, norm_g3, w_in_b, w_px, l, prompt, rope_tables=None, caches=None, cast_next=()):
    t = x.shape[0]
    tm = TOKEN_TILE
    steps = t // tm
    if prompt:
        row = lambda i: 0
    else:
        row = lambda i: 1 + (i * tm) // DEC_SEQ
    in_specs = [pl.BlockSpec((tm, D_MODEL), lambda i: (i, 0)),
                pl.BlockSpec((None, None, 1, 3 * D_MODEL), lambda i: (l, row(i), 0, 0)),
                pl.BlockSpec((None, 1, D_MODEL), lambda i: (l, 0, 0)),
                _resident((D_MODEL, D_IN_PROJ), lambda i: (0, 0)),
                _resident((None, D_MODEL, D_BRANCH), lambda i: (l, 0, 0))]
    args = [x, mod4, norm_g3, w_in_b, w_px]
    in_specs += [pl.BlockSpec((None, w.shape[1] // steps, w.shape[2]), lambda i: (l + 1, i, 0)) for w in cast_next]
    args += list(cast_next)
    act = jax.ShapeDtypeStruct((t, D_ACT), BF16)
    act_spec = pl.BlockSpec((tm, D_ACT), lambda i: (i, 0))
    out_shape = [act, act]
    out_specs = [act_spec, act_spec]
    aliases = {}
    if prompt:
        cache = jax.ShapeDtypeStruct((BATCH, DEPTH, N_ATT_HEADS, SEQ, HEAD_W), F32)
        cache_spec = pl.BlockSpec((tm // SEQ, None, N_ATT_HEADS, SEQ, HEAD_W), lambda i: (i, l, 0, 0, 0))
        out_shape += [cache, cache]
        out_specs += [cache_spec, cache_spec]
        if caches is not None:
            aliases = {len(args): 2, len(args) + 1: 3}
            in_specs += [pl.BlockSpec(memory_space=pl.ANY)] * 2
            args += list(caches)
    else:
        nblk = DEC_SEQ // tm
        in_specs += [pl.BlockSpec((tm, HEAD_W), lambda i: (i % nblk, 0))] * 3
        args += list(rope_tables)
    out_shape += [jax.ShapeDtypeStruct(w.shape[1:], BF16) for w in cast_next]
    out_specs += [pl.BlockSpec((w.shape[1] // steps, w.shape[2]), lambda i: (i, 0)) for w in cast_next]
    return pl.pallas_call(
        functools.partial(_inproj_kernel, prompt=prompt, aliased_caches=caches is not None,
                          n_cast=len(cast_next)),
        out_shape=out_shape,
        grid=(steps,),
        in_specs=in_specs,
        out_specs=out_specs,
        scratch_shapes=[pltpu.VMEM((tm, D_MODEL), BF16)],
        input_output_aliases=aliases,
        compiler_params=pltpu.CompilerParams(dimension_semantics=("arbitrary",), vmem_limit_bytes=VMEM_LIMIT),
        name="inproj_ctx" if prompt else "inproj_lat",
    )(*args)


def _make_mixer_kernel(s, l, with_past, final):
    lam_init = 0.8 - 0.6 * math.exp(-0.3 * l)
    past = PAST_LEN if with_past else 0
    sk = s + past
    nrb = s // ROW_BLOCK
    rb_rows = ROW_BLOCK
    n_out_chunks = D_MODEL // D_BRANCH

    def kern(*refs):
        it = iter(refs)
        us_ref, ur_ref, x_ref, xl_ref, mod_ref, wo_ref, fg_ref = (next(it) for _ in range(7))
        cs_ref, wf_ref, tp_ref, ps_ref = (next(it) for _ in range(4))
        dl_ref, sg_ref, fwd_ref, inv_ref, spec_ref, dwb_ref, lng_ref, lnb_ref, wpw_ref = (
            next(it) for _ in range(9))
        if with_past:
            ck_ref, cv_ref = next(it), next(it)
        o_ref = next(it)
        ppad_ref, gpad_ref, kall_ref, vx_ref, zcur_ref, zin_ref = (next(it) for _ in range(6))

        n = pl.program_id(0)
        if nrb == 1:
            r0 = 0
        else:
            r0 = pl.multiple_of((n % nrb) * rb_rows, rb_rows)

        def cols(c0, g=0, w=D_BRANCH):
            return slice(c0 + g * w, c0 + (g + 1) * w)

        def rows_at(base, size):
            return pl.ds(base, size) if nrb > 1 else slice(base, base + size)

        def stage_sequence():
            for pad_ref, col0 in ((ppad_ref, S_PX), (gpad_ref, S_GLU)):
                pad_ref[0:CONV_PAD, :] = jnp.zeros((CONV_PAD, D_BRANCH), BF16)
                pad_ref[CONV_PAD + s:CONV_PAD + s + CONV_TAIL, :] = jnp.zeros((CONV_TAIL, D_BRANCH), BF16)
                pad_ref[CONV_PAD:CONV_PAD + s, :] = us_ref[:, cols(col0)]
            ones = jnp.ones((sk, HEAD_W), BF16)
            for h in range(N_ATT_HEADS):
                if with_past:
                    kall_ref[h, 0:past, :] = ck_ref[h].astype(BF16)
                    vx_ref[h, 0:past, 0:HEAD_W] = cv_ref[h].astype(BF16)
                kall_ref[h, past:sk, :] = us_ref[:, cols(S_K, h, HEAD_W)]
                vx_ref[h, past:sk, 0:HEAD_W] = us_ref[:, cols(S_V, h, HEAD_W)]
                vx_ref[h, :, HEAD_W:2 * HEAD_W] = ones

        if nrb == 1:
            stage_sequence()
        else:
            pl.when(n % nrb == 0)(stage_sequence)

        @pl.when(n == 0)
        def _():
            zcur_ref[...] = jnp.zeros(zcur_ref.shape, BF16)

        zin_ref[...] = zcur_ref[...]

        def out_projection(z_src, x_src, row0):
            rows = pl.ds(row0, rb_rows)
            for c in range(n_out_chunks):
                y = jnp.dot(z_src[...], wo_ref[:, cols(0, c)], preferred_element_type=F32)
                gate = mod_ref[:, 2 * D_MODEL + c * D_BRANCH:2 * D_MODEL + (c + 1) * D_BRANCH]
                o_ref[rows, cols(0, c)] = x_src[:, cols(0, c)] + gate * y
            if final:
                ssq = jnp.zeros((rb_rows, 1), F32)
                for c in range(n_out_chunks):
                    out = o_ref[rows, cols(0, c)]
                    ssq = ssq + jnp.sum(out * out, axis=-1, keepdims=True)
                inv = lax.rsqrt(ssq * (1.0 / D_MODEL) + EPS)
                for c in range(n_out_chunks):
                    o_ref[rows, cols(0, c)] = o_ref[rows, cols(0, c)] * inv * fg_ref[:, cols(0, c)]

        def fourier():
            fx = us_ref[:, cols(S_FX)]
            p = jnp.dot(cs_ref[rows_at(r0, rb_rows), :], fx, preferred_element_type=F32)
            q = jnp.dot(cs_ref[rows_at(s + r0, rb_rows), :], fx, preferred_element_type=F32)
            yf = (jnp.dot(p.astype(BF16), wf_ref[0:D_BRANCH, :], preferred_element_type=F32)
                  + jnp.dot(q.astype(BF16), wf_ref[D_BRANCH:2 * D_BRANCH, :], preferred_element_type=F32))
            zcur_ref[:, cols(0)] = (yf * ur_ref[:, cols(R_FG)].astype(F32)).astype(BF16)

        def pooling():
            for k in range(rb_rows // CONV_OUT):
                base = r0 + k * CONV_OUT
                t = base + lax.broadcasted_iota(jnp.int32, (CONV_OUT, POOL_GROUP), 0)
                out_rows = slice(k * CONV_OUT, (k + 1) * CONV_OUT)
                for g, w in enumerate(POOL_WINDOWS):
                    win = ppad_ref[rows_at(base, CONV_DFT), cols(0, g, POOL_GROUP)]
                    ssum = jnp.dot(tp_ref[g], win, preferred_element_type=F32)
                    cnt = (jnp.minimum(t + w // 2, s) - jnp.maximum(t - w // 2, 0)).astype(F32)
                    px = ppad_ref[rows_at(base + CONV_PAD, CONV_OUT), cols(0, g, POOL_GROUP)].astype(F32)
                    pg = ur_ref[out_rows, cols(R_PG, g, POOL_GROUP)].astype(F32)
                    zcur_ref[out_rows, cols(D_BRANCH, g, POOL_GROUP)] = (
                        (ssum / cnt - px) * ps_ref[:, cols(0, g, POOL_GROUP)] * pg).astype(BF16)

        dl = dl_ref[...]
        lam = (jnp.exp(jnp.sum(dl[0:1] * dl[1:2], axis=-1, keepdims=True))
               - jnp.exp(jnp.sum(dl[2:3] * dl[3:4], axis=-1, keepdims=True)) + lam_init)
        first_map = lax.broadcasted_iota(jnp.int32, (rb_rows, HEAD_W), 1) < DIFF_HEAD

        def attention(h):
            qf = ur_ref[:, cols(R_Q, h, HEAD_W)].astype(F32)
            qq = jnp.concatenate([jnp.where(first_map, qf, 0.0).astype(BF16),
                                  jnp.where(first_map, 0.0, qf).astype(BF16)], axis=0)
            sc = lax.dot_general(qq, kall_ref[h], (((1,), (1,)), ((), ())), preferred_element_type=F32)
            m = jnp.max(sc, axis=-1, keepdims=True)
            p = jnp.exp2((sc - m).astype(BF16))
            ov = jnp.dot(p, vx_ref[h], preferred_element_type=F32)
            o = (ov[0:rb_rows, 0:HEAD_W] / ov[0:rb_rows, HEAD_W:2 * HEAD_W]
                 - lam * (ov[rb_rows:2 * rb_rows, 0:HEAD_W] / ov[rb_rows:2 * rb_rows, HEAD_W:2 * HEAD_W]))
            o = o * lax.rsqrt(jnp.mean(o * o, axis=-1, keepdims=True) + EPS) * sg_ref[...] * (1.0 - lam_init)
            ag = ur_ref[:, cols(R_AG, h, HEAD_W)].astype(F32)
            zcur_ref[:, cols(2 * D_BRANCH, h, HEAD_W)] = (o * ag).astype(BF16)

        def conv_module():
            half = CONV_DFT // 2
            parts = []
            for k in range(rb_rows // CONV_OUT):
                win = gpad_ref[rows_at(r0 + k * CONV_OUT, CONV_DFT), :]
                spec = jnp.dot(fwd_ref[...], win, preferred_element_type=F32)
                yr, yi = spec[0:half], spec[half:CONV_DFT]
                zr = yr * spec_ref[0] - yi * spec_ref[1]
                zi = yr * spec_ref[1] + yi * spec_ref[2]
                z = jnp.concatenate([zr, zi], axis=0).astype(BF16)
                parts.append(jnp.dot(inv_ref[...], z, preferred_element_type=F32))
            y = jnp.concatenate(parts, axis=0) + dwb_ref[...]
            mu = jnp.mean(y, axis=-1, keepdims=True)
            d = y - mu
            var = jnp.mean(d * d, axis=-1, keepdims=True)
            act = _silu(d * lax.rsqrt(var + EPS) * lng_ref[...] + lnb_ref[...]).astype(BF16)
            yc = jnp.dot(act, wpw_ref[...], preferred_element_type=F32)
            zcur_ref[:, cols(3 * D_BRANCH)] = (yc * ur_ref[:, cols(R_CG)].astype(F32)).astype(BF16)

        out_projection(zin_ref, x_ref, pl.multiple_of(((n + 1) % 2) * rb_rows, rb_rows))
        conv_module()
        fourier()
        pooling()
        for h in range(N_ATT_HEADS):
            attention(h)

        @pl.when(n == pl.num_programs(0) - 1)
        def _():
            out_projection(zcur_ref, xl_ref, rb_rows)

    return kern


def _mixer(us, ur, x, mod4, w_out_b, final_g2, consts, params, l, s, prompt, final, cache_k=None, cache_v=None):
    with_past = cache_k is not None
    nrb = s // ROW_BLOCK
    n_blocks = x.shape[0] // ROW_BLOCK
    sk = s + (PAST_LEN if with_past else 0)
    wfold, pool_scale3, diff_lambda, subln3, conv_spec, dwb3, lng3, lnb3, w_pw_b = params

    assert (prompt or nrb >= 2) and n_blocks % 2 == 0

    def prev(n):
        return jnp.maximum(n - 1, 0)

    def mod_row(n):
        return 0 if prompt else 1 + prev(n) // nrb

    def per_layer(shape):
        nd = len(shape)
        return pl.BlockSpec((None,) + shape, lambda n: (l,) + (0,) * nd)

    in_specs = [pl.BlockSpec((s, D_ACT), lambda n: (n // nrb, 0)),
                pl.BlockSpec((ROW_BLOCK, D_ACT), lambda n: (n, 0)),
                pl.BlockSpec((ROW_BLOCK, D_MODEL), lambda n: (prev(n), 0)),
                pl.BlockSpec((ROW_BLOCK, D_MODEL), lambda n: (n_blocks - 1, 0)),
                pl.BlockSpec((None, None, 1, 3 * D_MODEL), lambda n: (l, mod_row(n), 0, 0)),
                _resident((D_MODEL, D_MODEL), lambda n: (0, 0)),
                pl.BlockSpec((1, D_MODEL), lambda n: (0, 0)),
                _resident((2 * s, s), lambda n: (0, 0)),
                _resident((None, 2 * D_BRANCH, D_BRANCH), lambda n: (l, 0, 0)),
                _resident((len(POOL_WINDOWS), CONV_OUT, CONV_DFT), lambda n: (0, 0, 0)),
                per_layer((1, D_BRANCH)),
                per_layer((4, DIFF_HEAD)),
                per_layer((1, HEAD_W)),
                _resident((CONV_DFT, CONV_DFT), lambda n: (0, 0)),
                _resident((CONV_OUT, CONV_DFT), lambda n: (0, 0)),
                _resident((None, 3, CONV_DFT // 2, D_BRANCH), lambda n: (l, 0, 0, 0)),
                per_layer((1, D_BRANCH)),
                per_layer((1, D_BRANCH)),
                per_layer((1, D_BRANCH)),
                _resident((None, D_BRANCH, D_BRANCH), lambda n: (l, 0, 0))]
    args = [us, ur, x, x, mod4, w_out_b, final_g2, consts["cs"], wfold, consts["tpool"], pool_scale3,
            diff_lambda, subln3, consts["conv_fwd"], consts["conv_inv"], conv_spec, dwb3, lng3, lnb3, w_pw_b]
    if with_past:
        cache_spec = pl.BlockSpec((None, None, N_ATT_HEADS, PAST_LEN, HEAD_W),
                                  lambda n: (n // nrb, l, 0, 0, 0))
        in_specs += [cache_spec, cache_spec]
        args += [cache_k, cache_v]
    scratch = [pltpu.VMEM((CONV_PAD + s + CONV_TAIL, D_BRANCH), BF16),
               pltpu.VMEM((CONV_PAD + s + CONV_TAIL, D_BRANCH), BF16),
               pltpu.VMEM((N_ATT_HEADS, sk, HEAD_W), BF16),
               pltpu.VMEM((N_ATT_HEADS, sk, 2 * HEAD_W), BF16),
               pltpu.VMEM((ROW_BLOCK, D_MODEL), BF16),
               pltpu.VMEM((ROW_BLOCK, D_MODEL), BF16)]
    return pl.pallas_call(
        _make_mixer_kernel(s, l, with_past, final),
        out_shape=jax.ShapeDtypeStruct(x.shape, F32),
        grid=(n_blocks,),
        in_specs=in_specs,
        out_specs=pl.BlockSpec((2 * ROW_BLOCK, D_MODEL), lambda n: (prev(n) // 2, 0)),
        scratch_shapes=scratch,
        compiler_params=pltpu.CompilerParams(dimension_semantics=("arbitrary",), vmem_limit_bytes=VMEM_LIMIT),
        name="mixer_ctx" if prompt else "mixer_lat",
    )(*args)


def kernel(x_prompt, x_sample, cache_k, cache_v, c, c_ctx, norm_g, w_mod, b_mod, w_in, w_fourier, w_pool,
           pool_scale, diff_lambda, subln_g, conv_dw, conv_dw_b, conv_ln_g, conv_ln_b, w_conv_pw, w_out, final_g):
    w_in_b = w_in[0].astype(BF16)
    w_out_b = w_out[0].astype(BF16)
    w_pw_b = w_conv_pw.astype(BF16)
    w_px = _fold_pool(w_in, w_pool)
    wfold = _fold_fourier(w_fourier)

    cv = jnp.concatenate([c_ctx[None, :], c, jnp.zeros((MOD_ROWS - 1 - DEC_BATCH, D_MODEL), F32)], axis=0)
    mod4 = _modulation(cv, w_mod, b_mod).reshape(DEPTH, MOD_ROWS, 1, 3 * D_MODEL)

    norm_g3 = norm_g.reshape(DEPTH, 1, D_MODEL)
    final_g2 = final_g.reshape(1, D_MODEL)
    params = (wfold, pool_scale.reshape(DEPTH, 1, D_BRANCH), diff_lambda,
              subln_g.reshape(DEPTH, 1, HEAD_W),
              _conv_spectrum(conv_dw),
              conv_dw_b.reshape(DEPTH, 1, D_BRANCH),
              conv_ln_g.reshape(DEPTH, 1, D_BRANCH), conv_ln_b.reshape(DEPTH, 1, D_BRANCH), w_pw_b)

    tpool = jnp.asarray(_pool_windows()).astype(BF16)
    conv_fwd, conv_inv = (jnp.asarray(m).astype(BF16) for m in _conv_dft())
    shared = {"tpool": tpool, "conv_fwd": conv_fwd, "conv_inv": conv_inv}
    consts_ctx = dict(shared, cs=jnp.asarray(_dft_positions(SEQ)).astype(BF16))
    consts_lat = dict(shared, cs=jnp.asarray(_dft_positions(DEC_SEQ)).astype(BF16))
    rope_tables = tuple(jnp.asarray(t) for t in _rope_tables(DEC_SEQ))

    xp = x_prompt.reshape(BATCH * SEQ, D_MODEL)
    xs = x_sample.reshape(DEC_BATCH * DEC_SEQ, D_MODEL)
    caches = None
    for l in range(DEPTH):
        final = l == DEPTH - 1
        cast_next = () if final else (w_in, w_out)
        us, ur, kc, vc, *next_weights = _inproj(xp, mod4, norm_g3, w_in_b, w_px, l, prompt=True, caches=caches,
                                                cast_next=cast_next)
        caches = (kc, vc)
        xp = _mixer(us, ur, xp, mod4, w_out_b, final_g2, consts_ctx, params, l, SEQ, prompt=True, final=final)

        us, ur = _inproj(xs, mod4, norm_g3, w_in_b, w_px, l, prompt=False, rope_tables=rope_tables)
        xs = _mixer(us, ur, xs, mod4, w_out_b, final_g2, consts_lat, params, l, DEC_SEQ, prompt=False,
                    final=final, cache_k=cache_k, cache_v=cache_v)
        if next_weights:
            w_in_b, w_out_b = next_weights

    return (xp.reshape(BATCH, SEQ, D_MODEL), xs.reshape(DEC_BATCH, DEC_SEQ, D_MODEL), caches[0], caches[1])
```

```python
import functools
import math

import numpy as np
import jax
import jax.numpy as jnp
from jax import lax
from jax.experimental import pallas as pl
from jax.experimental.pallas import tpu as pltpu

D_MODEL = 2048
BATCH = 32
SEQ = 256
DEPTH = 2
DEC_BATCH = 8
DEC_SEQ = 1024
PAST_LEN = 256
GRID_W = 64
D_BRANCH = 512
FFT_GROUP = 128
POOL_WINDOWS = (2, 4, 8, 16)
POOL_GROUP = 128
N_ATT_HEADS = 4
DIFF_HEAD = 64
HEAD_W = 128
N_FREQ = 16
CONV_WIDTH = 31
ROPE_BASE = 10000.0
EPS = 1e-6
N_IN_CHUNKS = 11
D_IN_PROJ = N_IN_CHUNKS * D_BRANCH
MOD_ROWS = 16

W_FX, W_FG, W_PX, W_PG, W_Q, W_K, W_V, W_AG, W_CA, W_CB, W_CG = range(N_IN_CHUNKS)
D_ACT = 5 * D_BRANCH
S_FX, S_PX, S_K, S_V, S_GLU = (i * D_BRANCH for i in range(5))
R_FG, R_PG, R_Q, R_AG, R_CG = (i * D_BRANCH for i in range(5))

Q_SCALE = DIFF_HEAD ** -0.5 * math.log2(math.e)

ROW_BLOCK = 256
CONV_PAD = 16
CONV_DFT = 256
CONV_OUT = 128
CONV_TAIL = CONV_DFT - CONV_OUT - CONV_PAD
CONV_TAPS_PAD = 32
TOKEN_TILE = 512
VMEM_LIMIT = 62 * 1024 * 1024

F32 = jnp.float32
BF16 = jnp.bfloat16


def _sigmoid(x):
    return 0.5 * jnp.tanh(0.5 * x) + 0.5


def _silu(x):
    return x * _sigmoid(x)


def _resident(block_shape, index_map):
    return pl.BlockSpec(block_shape, index_map, pipeline_mode=pl.Buffered(1))


@functools.lru_cache(maxsize=None)
def _dft_positions(s):
    k = np.arange(s, dtype=np.int64)
    ang = 2.0 * np.pi * ((k[:, None] * k[None, :]) % s) / s
    return (np.concatenate([np.cos(ang), np.sin(ang)], axis=0) / np.sqrt(s)).astype(np.float32)


@functools.lru_cache(maxsize=None)
def _dft_channels():
    k = np.arange(FFT_GROUP, dtype=np.int64)
    ang = 2.0 * np.pi * ((k[:, None] * k[None, :]) % FFT_GROUP) / FFT_GROUP
    return (np.stack([np.cos(ang), -np.sin(ang)]) / np.sqrt(FFT_GROUP)).astype(np.float32)


@functools.lru_cache(maxsize=None)
def _pool_windows():
    i = np.arange(CONV_OUT)[:, None]
    j = np.arange(CONV_DFT)[None, :] - CONV_PAD
    return np.stack([((j >= i - w // 2) & (j < i + w // 2)) for w in POOL_WINDOWS]).astype(np.float32)


@functools.lru_cache(maxsize=None)
def _rope_tables(s):
    rows = s // GRID_W
    t_row = np.repeat(np.arange(rows), GRID_W).astype(np.float64)
    t_col = np.tile(np.arange(GRID_W), rows).astype(np.float64)
    inv = ROPE_BASE ** (-np.arange(N_FREQ, dtype=np.float64) / N_FREQ)
    lane = np.arange(HEAD_W)
    d = lane % DIFF_HEAD
    axis = d // (2 * N_FREQ)
    half = (d % (2 * N_FREQ)) // N_FREQ
    freq = d % N_FREQ
    pos = np.where(axis[None, :] == 0, t_row[:, None], t_col[:, None])
    ang = pos * inv[freq][None, :]
    cos, sin = np.cos(ang), np.sin(ang)
    sin_a = np.where(half[None, :] == 0, -sin, 0.0)
    sin_b = np.where(half[None, :] == 1, sin, 0.0)
    return cos.astype(np.float32), sin_a.astype(np.float32), sin_b.astype(np.float32)


@functools.lru_cache(maxsize=None)
def _conv_dft():
    n, half = CONV_DFT, CONV_DFT // 2
    f = np.arange(half, dtype=np.int64)[:, None]
    u = np.arange(n, dtype=np.int64)[None, :]
    ang = 2.0 * np.pi * ((f * u) % n) / n
    fwd = np.concatenate([np.cos(ang), -np.sin(ang)], axis=0)
    fwd[half, :] = 1.0 - 2.0 * (u[0] % 2)
    t = np.arange(CONV_OUT, dtype=np.int64)[:, None]
    ang = 2.0 * np.pi * ((t * f.T) % n) / n
    inv_re = 2.0 * np.cos(ang) / n
    inv_re[:, 0] = 1.0 / n
    inv_im = -2.0 * np.sin(ang) / n
    inv_im[:, 0] = (1.0 - 2.0 * (t[:, 0] % 2)) / n
    return fwd.astype(np.float32), np.concatenate([inv_re, inv_im], axis=1).astype(np.float32)


@functools.lru_cache(maxsize=None)
def _conv_spectrum_trig():
    n, half = CONV_DFT, CONV_DFT // 2
    f = np.arange(half, dtype=np.int64)[:, None]
    lag = (np.arange(CONV_TAPS_PAD, dtype=np.int64) + CONV_PAD - CONV_WIDTH // 2)[None, :]
    ang = 2.0 * np.pi * ((f * lag) % n) / n
    cos, sin = np.cos(ang), np.sin(ang)
    cos_nyq = cos.copy()
    cos_nyq[0, :] = 1.0 - 2.0 * (lag[0] % 2)
    return np.stack([cos, sin, cos_nyq]).astype(np.float32)


def _spectrum_kernel(trig_ref, dw_ref, o_ref):
    for k in range(3):
        o_ref[k] = jnp.dot(trig_ref[k], dw_ref[...], preferred_element_type=F32, precision=lax.Precision.HIGHEST)


def _conv_spectrum(conv_dw):
    half = CONV_DFT // 2
    dw_pad = jnp.pad(conv_dw, ((0, 0), (0, CONV_TAPS_PAD - CONV_WIDTH), (0, 0)))
    return pl.pallas_call(
        _spectrum_kernel,
        out_shape=jax.ShapeDtypeStruct((DEPTH, 3, half, D_BRANCH), F32),
        grid=(DEPTH,),
        in_specs=[pl.BlockSpec((3, half, CONV_TAPS_PAD), lambda l: (0, 0, 0)),
                  pl.BlockSpec((None, CONV_TAPS_PAD, D_BRANCH), lambda l: (l, 0, 0))],
        out_specs=pl.BlockSpec((None, 3, half, D_BRANCH), lambda l: (l, 0, 0, 0)),
        name="conv_spectrum",
    )(jnp.asarray(_conv_spectrum_trig()), dw_pad)


def _fold_kernel(dft_ref, wf_ref, o_ref):
    for part in range(2):
        for g in range(D_BRANCH // FFT_GROUP):
            rows = slice(g * FFT_GROUP, (g + 1) * FFT_GROUP)
            prod = jnp.dot(dft_ref[part], wf_ref[rows, :], preferred_element_type=F32,
                           precision=lax.Precision.HIGHEST)
            o_ref[part * D_BRANCH + g * FFT_GROUP:part * D_BRANCH + (g + 1) * FFT_GROUP, :] = prod.astype(BF16)


def _fold_fourier(w_fourier):
    dft = jnp.asarray(_dft_channels())
    return pl.pallas_call(
        _fold_kernel,
        out_shape=jax.ShapeDtypeStruct((DEPTH, 2 * D_BRANCH, D_BRANCH), BF16),
        grid=(DEPTH,),
        in_specs=[pl.BlockSpec((2, FFT_GROUP, FFT_GROUP), lambda l: (0, 0, 0)),
                  pl.BlockSpec((None, D_BRANCH, D_BRANCH), lambda l: (l, 0, 0))],
        out_specs=pl.BlockSpec((None, 2 * D_BRANCH, D_BRANCH), lambda l: (l, 0, 0)),
        name="fold_fourier",
    )(dft, w_fourier)


def _fold_pool_kernel(wpx_ref, wp_ref, o_ref):
    for g in range(len(POOL_WINDOWS)):
        gc = slice(g * POOL_GROUP, (g + 1) * POOL_GROUP)
        o_ref[:, gc] = jnp.dot(wpx_ref[:, gc].astype(BF16), wp_ref[g].astype(BF16),
                               preferred_element_type=F32).astype(BF16)


def _fold_pool(w_in, w_pool):
    return pl.pallas_call(
        _fold_pool_kernel,
        out_shape=jax.ShapeDtypeStruct((DEPTH, D_MODEL, D_BRANCH), BF16),
        grid=(DEPTH,),
        in_specs=[pl.BlockSpec((None, D_MODEL, D_BRANCH), lambda l: (l, 0, W_PX)),
                  pl.BlockSpec((None, len(POOL_WINDOWS), POOL_GROUP, POOL_GROUP), lambda l: (l, 0, 0, 0))],
        out_specs=pl.BlockSpec((None, D_MODEL, D_BRANCH), lambda l: (l, 0, 0)),
        name="fold_pool",
    )(w_in, w_pool)


def _mod_kernel(cv_ref, w_ref, b_ref, o_ref):
    a = _silu(cv_ref[...]).astype(BF16)
    o_ref[...] = jnp.dot(a, w_ref[...].astype(BF16), preferred_element_type=F32) + b_ref[...]


def _modulation(cv, w_mod, b_mod):
    tn = 1024
    return pl.pallas_call(
        _mod_kernel,
        out_shape=jax.ShapeDtypeStruct((DEPTH, MOD_ROWS, 3 * D_MODEL), F32),
        grid=(DEPTH, 3 * D_MODEL // tn),
        in_specs=[pl.BlockSpec((MOD_ROWS, D_MODEL), lambda l, j: (0, 0)),
                  pl.BlockSpec((None, D_MODEL, tn), lambda l, j: (l, 0, j)),
                  pl.BlockSpec((None, 1, tn), lambda l, j: (l, 0, j))],
        out_specs=pl.BlockSpec((None, MOD_ROWS, tn), lambda l, j: (l, 0, j)),
        compiler_params=pltpu.CompilerParams(vmem_limit_bytes=VMEM_LIMIT),
        name="modulation",
    )(cv, w_mod, b_mod.reshape(DEPTH, 1, 3 * D_MODEL))


def _rope(x, cos, sin_a, sin_b):
    return x * cos + pltpu.roll(x, HEAD_W - N_FREQ, 1) * sin_a + pltpu.roll(x, N_FREQ, 1) * sin_b


def _inproj_kernel(*refs, prompt, aliased_caches, n_cast):
    it = iter(refs)
    x_ref, mod_ref, g_ref, w_ref, wpx_ref = (next(it) for _ in range(5))
    cast_src = [next(it) for _ in range(n_cast)]
    if prompt:
        if aliased_caches:
            next(it), next(it)
        us_ref, ur_ref, kc_ref, vc_ref = (next(it) for _ in range(4))
    else:
        cos_ref, sa_ref, sb_ref, us_ref, ur_ref = (next(it) for _ in range(5))
    cast_dst = [next(it) for _ in range(n_cast)]
    h_ref = next(it)

    for src, dst in zip(cast_src, cast_dst):
        dst[...] = src[...].astype(BF16)

    tm = x_ref.shape[0]
    shift = mod_ref[:, 0:D_MODEL]
    scale = mod_ref[:, D_MODEL:2 * D_MODEL]
    for r in range(tm // 128):
        rows = slice(r * 128, (r + 1) * 128)
        x = x_ref[rows, :]
        y = x * lax.rsqrt(jnp.mean(x * x, axis=-1, keepdims=True) + EPS) * g_ref[...]
        h_ref[rows, :] = (y * (1.0 + scale) + shift).astype(BF16)

    def proj(c):
        return jnp.dot(h_ref[...], w_ref[:, c * D_BRANCH:(c + 1) * D_BRANCH], preferred_element_type=F32)

    def put(dst, col, val):
        dst[:, col:col + val.shape[1]] = val.astype(BF16)

    def put_cache(dst, val):
        for bb in range(tm // SEQ):
            for h in range(N_ATT_HEADS):
                dst[bb, h] = val[bb * SEQ:(bb + 1) * SEQ, h * HEAD_W:(h + 1) * HEAD_W]

    put(us_ref, S_FX, proj(W_FX))
    put(ur_ref, R_FG, _silu(proj(W_FG)))
    put(us_ref, S_PX, jnp.dot(h_ref[...], wpx_ref[...], preferred_element_type=F32))
    put(ur_ref, R_PG, _silu(proj(W_PG)))
    q = proj(W_Q)
    k = proj(W_K)
    if prompt:
        put_cache(kc_ref, k)
        put(ur_ref, R_Q, q * Q_SCALE)
        put(us_ref, S_K, k)
    else:
        cos, sa, sb = cos_ref[...], sa_ref[...], sb_ref[...]
        for h in range(N_ATT_HEADS):
            hc = slice(h * HEAD_W, (h + 1) * HEAD_W)
            put(ur_ref, R_Q + h * HEAD_W, _rope(q[:, hc], cos, sa, sb) * Q_SCALE)
            put(us_ref, S_K + h * HEAD_W, _rope(k[:, hc], cos, sa, sb))
    v = proj(W_V)
    if prompt:
        put_cache(vc_ref, v)
    put(us_ref, S_V, v)
    put(ur_ref, R_AG, _silu(proj(W_AG)))
    put(us_ref, S_GLU, proj(W_CA) * _sigmoid(proj(W_CB)))
    put(ur_ref, R_CG, _silu(proj(W_CG)))


def _inproj(x, mod4, norm_g3, w_in_b, w_px, l, prompt, rope_tables=None, caches=None, casts=()):
    t = x.shape[0]
    tm = TOKEN_TILE
    steps = t // tm
    if prompt:
        row = lambda i: 0
    else:
        row = lambda i: 1 + (i * tm) // DEC_SEQ
    in_specs = [pl.BlockSpec((tm, D_MODEL), lambda i: (i, 0)),
                pl.BlockSpec((None, None, 1, 3 * D_MODEL), lambda i: (l, row(i), 0, 0)),
                pl.BlockSpec((None, 1, D_MODEL), lambda i: (l, 0, 0)),
                _resident((D_MODEL, D_IN_PROJ), lambda i: (0, 0)),
                _resident((None, D_MODEL, D_BRANCH), lambda i: (l, 0, 0))]
    args = [x, mod4, norm_g3, w_in_b, w_px]
    in_specs += [pl.BlockSpec((None, w.shape[1] // steps, w.shape[2]), lambda i, wl=wl: (wl, i, 0))
                 for w, wl in casts]
    args += [w for w, _ in casts]
    act = jax.ShapeDtypeStruct((t, D_ACT), BF16)
    act_spec = pl.BlockSpec((tm, D_ACT), lambda i: (i, 0))
    out_shape = [act, act]
    out_specs = [act_spec, act_spec]
    aliases = {}
    if prompt:
        cache = jax.ShapeDtypeStruct((BATCH, DEPTH, N_ATT_HEADS, SEQ, HEAD_W), F32)
        cache_spec = pl.BlockSpec((tm // SEQ, None, N_ATT_HEADS, SEQ, HEAD_W), lambda i: (i, l, 0, 0, 0))
        out_shape += [cache, cache]
        out_specs += [cache_spec, cache_spec]
        if caches is not None:
            aliases = {len(args): 2, len(args) + 1: 3}
            in_specs += [pl.BlockSpec(memory_space=pl.ANY)] * 2
            args += list(caches)
    else:
        nblk = DEC_SEQ // tm
        in_specs += [pl.BlockSpec((tm, HEAD_W), lambda i: (i % nblk, 0))] * 3
        args += list(rope_tables)
    out_shape += [jax.ShapeDtypeStruct(w.shape[1:], BF16) for w, _ in casts]
    out_specs += [pl.BlockSpec((w.shape[1] // steps, w.shape[2]), lambda i: (i, 0)) for w, _ in casts]
    return pl.pallas_call(
        functools.partial(_inproj_kernel, prompt=prompt, aliased_caches=caches is not None,
                          n_cast=len(casts)),
        out_shape=out_shape,
        grid=(steps,),
        in_specs=in_specs,
        out_specs=out_specs,
        scratch_shapes=[pltpu.VMEM((tm, D_MODEL), BF16)],
        input_output_aliases=aliases,
        compiler_params=pltpu.CompilerParams(dimension_semantics=("arbitrary",), vmem_limit_bytes=VMEM_LIMIT),
        name="inproj_ctx" if prompt else "inproj_lat",
    )(*args)


def _make_mixer_kernel(s, l, with_past, final):
    lam_init = 0.8 - 0.6 * math.exp(-0.3 * l)
    past = PAST_LEN if with_past else 0
    sk = s + past
    nrb = s // ROW_BLOCK
    rb_rows = ROW_BLOCK
    n_out_chunks = D_MODEL // D_BRANCH

    def kern(*refs):
        it = iter(refs)
        us_ref, ur_ref, x_ref, xl_ref, mod_ref, wo_ref, fg_ref = (next(it) for _ in range(7))
        cs_ref, wf_ref, tp_ref, ps_ref = (next(it) for _ in range(4))
        dl_ref, sg_ref, fwd_ref, inv_ref, spec_ref, dwb_ref, lng_ref, lnb_ref, wpw_ref = (
            next(it) for _ in range(9))
        if with_past:
            ck_ref, cv_ref = next(it), next(it)
        o_ref = next(it)
        ppad_ref, gpad_ref, kall_ref, vx_ref, zcur_ref, zin_ref = (next(it) for _ in range(6))

        n = pl.program_id(0)
        if nrb == 1:
            r0 = 0
        else:
            r0 = pl.multiple_of((n % nrb) * rb_rows, rb_rows)

        def cols(c0, g=0, w=D_BRANCH):
            return slice(c0 + g * w, c0 + (g + 1) * w)

        def rows_at(base, size):
            return pl.ds(base, size) if nrb > 1 else slice(base, base + size)

        def stage_sequence():
            for pad_ref, col0 in ((ppad_ref, S_PX), (gpad_ref, S_GLU)):
                pad_ref[0:CONV_PAD, :] = jnp.zeros((CONV_PAD, D_BRANCH), BF16)
                pad_ref[CONV_PAD + s:CONV_PAD + s + CONV_TAIL, :] = jnp.zeros((CONV_TAIL, D_BRANCH), BF16)
                pad_ref[CONV_PAD:CONV_PAD + s, :] = us_ref[:, cols(col0)]
            ones = jnp.ones((sk, HEAD_W), BF16)
            for h in range(N_ATT_HEADS):
                if with_past:
                    kall_ref[h, 0:past, :] = ck_ref[h].astype(BF16)
                    vx_ref[h, 0:past, 0:HEAD_W] = cv_ref[h].astype(BF16)
                kall_ref[h, past:sk, :] = us_ref[:, cols(S_K, h, HEAD_W)]
                vx_ref[h, past:sk, 0:HEAD_W] = us_ref[:, cols(S_V, h, HEAD_W)]
                vx_ref[h, :, HEAD_W:2 * HEAD_W] = ones

        if nrb == 1:
            stage_sequence()
        else:
            pl.when(n % nrb == 0)(stage_sequence)

        @pl.when(n == 0)
        def _():
            zcur_ref[...] = jnp.zeros(zcur_ref.shape, BF16)

        zin_ref[...] = zcur_ref[...]

        def out_projection(z_src, x_src, row0):
            rows = pl.ds(row0, rb_rows)
            for c in range(n_out_chunks):
                y = jnp.dot(z_src[...], wo_ref[:, cols(0, c)], preferred_element_type=F32)
                gate = mod_ref[:, 2 * D_MODEL + c * D_BRANCH:2 * D_MODEL + (c + 1) * D_BRANCH]
                o_ref[rows, cols(0, c)] = x_src[:, cols(0, c)] + gate * y
            if final:
                ssq = jnp.zeros((rb_rows, 1), F32)
                for c in range(n_out_chunks):
                    out = o_ref[rows, cols(0, c)]
                    ssq = ssq + jnp.sum(out * out, axis=-1, keepdims=True)
                inv = lax.rsqrt(ssq * (1.0 / D_MODEL) + EPS)
                for c in range(n_out_chunks):
                    o_ref[rows, cols(0, c)] = o_ref[rows, cols(0, c)] * inv * fg_ref[:, cols(0, c)]

        def fourier():
            fx = us_ref[:, cols(S_FX)]
            p = jnp.dot(cs_ref[rows_at(r0, rb_rows), :], fx, preferred_element_type=F32)
            q = jnp.dot(cs_ref[rows_at(s + r0, rb_rows), :], fx, preferred_element_type=F32)
            yf = (jnp.dot(p.astype(BF16), wf_ref[0:D_BRANCH, :], preferred_element_type=F32)
                  + jnp.dot(q.astype(BF16), wf_ref[D_BRANCH:2 * D_BRANCH, :], preferred_element_type=F32))
            zcur_ref[:, cols(0)] = (yf * ur_ref[:, cols(R_FG)].astype(F32)).astype(BF16)

        def pooling():
            for k in range(rb_rows // CONV_OUT):
                base = r0 + k * CONV_OUT
                t = base + lax.broadcasted_iota(jnp.int32, (CONV_OUT, POOL_GROUP), 0)
                out_rows = slice(k * CONV_OUT, (k + 1) * CONV_OUT)
                for g, w in enumerate(POOL_WINDOWS):
                    win = ppad_ref[rows_at(base, CONV_DFT), cols(0, g, POOL_GROUP)]
                    ssum = jnp.dot(tp_ref[g], win, preferred_element_type=F32)
                    cnt = (jnp.minimum(t + w // 2, s) - jnp.maximum(t - w // 2, 0)).astype(F32)
                    px = ppad_ref[rows_at(base + CONV_PAD, CONV_OUT), cols(0, g, POOL_GROUP)].astype(F32)
                    pg = ur_ref[out_rows, cols(R_PG, g, POOL_GROUP)].astype(F32)
                    zcur_ref[out_rows, cols(D_BRANCH, g, POOL_GROUP)] = (
                        (ssum / cnt - px) * ps_ref[:, cols(0, g, POOL_GROUP)] * pg).astype(BF16)

        dl = dl_ref[...]
        lam = (jnp.exp(jnp.sum(dl[0:1] * dl[1:2], axis=-1, keepdims=True))
               - jnp.exp(jnp.sum(dl[2:3] * dl[3:4], axis=-1, keepdims=True)) + lam_init)
        first_map = lax.broadcasted_iota(jnp.int32, (rb_rows, HEAD_W), 1) < DIFF_HEAD

        def attention(h):
            qf = ur_ref[:, cols(R_Q, h, HEAD_W)].astype(F32)
            qq = jnp.concatenate([jnp.where(first_map, qf, 0.0).astype(BF16),
                                  jnp.where(first_map, 0.0, qf).astype(BF16)], axis=0)
            sc = lax.dot_general(qq, kall_ref[h], (((1,), (1,)), ((), ())), preferred_element_type=F32)
            m = jnp.max(sc, axis=-1, keepdims=True)
            p = jnp.exp2((sc - m).astype(BF16))
            ov = jnp.dot(p, vx_ref[h], preferred_element_type=F32)
            o = (ov[0:rb_rows, 0:HEAD_W] / ov[0:rb_rows, HEAD_W:2 * HEAD_W]
                 - lam * (ov[rb_rows:2 * rb_rows, 0:HEAD_W] / ov[rb_rows:2 * rb_rows, HEAD_W:2 * HEAD_W]))
            o = o * lax.rsqrt(jnp.mean(o * o, axis=-1, keepdims=True) + EPS) * sg_ref[...] * (1.0 - lam_init)
            ag = ur_ref[:, cols(R_AG, h, HEAD_W)].astype(F32)
            zcur_ref[:, cols(2 * D_BRANCH, h, HEAD_W)] = (o * ag).astype(BF16)

        def conv_module():
            half = CONV_DFT // 2
            parts = []
            for k in range(rb_rows // CONV_OUT):
                win = gpad_ref[rows_at(r0 + k * CONV_OUT, CONV_DFT), :]
                spec = jnp.dot(fwd_ref[...], win, preferred_element_type=F32)
                yr, yi = spec[0:half], spec[half:CONV_DFT]
                zr = yr * spec_ref[0] - yi * spec_ref[1]
                zi = yr * spec_ref[1] + yi * spec_ref[2]
                z = jnp.concatenate([zr, zi], axis=0).astype(BF16)
                parts.append(jnp.dot(inv_ref[...], z, preferred_element_type=F32))
            y = jnp.concatenate(parts, axis=0) + dwb_ref[...]
            mu = jnp.mean(y, axis=-1, keepdims=True)
            d = y - mu
            var = jnp.mean(d * d, axis=-1, keepdims=True)
            act = _silu(d * lax.rsqrt(var + EPS) * lng_ref[...] + lnb_ref[...]).astype(BF16)
            yc = jnp.dot(act, wpw_ref[...], preferred_element_type=F32)
            zcur_ref[:, cols(3 * D_BRANCH)] = (yc * ur_ref[:, cols(R_CG)].astype(F32)).astype(BF16)

        out_projection(zin_ref, x_ref, pl.multiple_of(((n + 1) % 2) * rb_rows, rb_rows))
        conv_module()
        fourier()
        pooling()
        for h in range(N_ATT_HEADS):
            attention(h)

        @pl.when(n == pl.num_programs(0) - 1)
        def _():
            out_projection(zcur_ref, xl_ref, rb_rows)

    return kern


def _mixer(us, ur, x, mod4, w_out_b, final_g2, consts, params, l, s, prompt, final, cache_k=None, cache_v=None):
    with_past = cache_k is not None
    nrb = s // ROW_BLOCK
    n_blocks = x.shape[0] // ROW_BLOCK
    sk = s + (PAST_LEN if with_past else 0)
    wfold, pool_scale3, diff_lambda, subln3, conv_spec, dwb3, lng3, lnb3, w_pw_b = params

    assert (prompt or nrb >= 2) and n_blocks % 2 == 0

    def prev(n):
        return jnp.maximum(n - 1, 0)

    def mod_row(n):
        return 0 if prompt else 1 + prev(n) // nrb

    def per_layer(shape):
        nd = len(shape)
        return pl.BlockSpec((None,) + shape, lambda n: (l,) + (0,) * nd)

    in_specs = [pl.BlockSpec((s, D_ACT), lambda n: (n // nrb, 0)),
                pl.BlockSpec((ROW_BLOCK, D_ACT), lambda n: (n, 0)),
                pl.BlockSpec((ROW_BLOCK, D_MODEL), lambda n: (prev(n), 0)),
                pl.BlockSpec((ROW_BLOCK, D_MODEL), lambda n: (n_blocks - 1, 0)),
                pl.BlockSpec((None, None, 1, 3 * D_MODEL), lambda n: (l, mod_row(n), 0, 0)),
                _resident((D_MODEL, D_MODEL), lambda n: (0, 0)),
                pl.BlockSpec((1, D_MODEL), lambda n: (0, 0)),
                _resident((2 * s, s), lambda n: (0, 0)),
                _resident((None, 2 * D_BRANCH, D_BRANCH), lambda n: (l, 0, 0)),
                _resident((len(POOL_WINDOWS), CONV_OUT, CONV_DFT), lambda n: (0, 0, 0)),
                per_layer((1, D_BRANCH)),
                per_layer((4, DIFF_HEAD)),
                per_layer((1, HEAD_W)),
                _resident((CONV_DFT, CONV_DFT), lambda n: (0, 0)),
                _resident((CONV_OUT, CONV_DFT), lambda n: (0, 0)),
                _resident((None, 3, CONV_DFT // 2, D_BRANCH), lambda n: (l, 0, 0, 0)),
                per_layer((1, D_BRANCH)),
                per_layer((1, D_BRANCH)),
                per_layer((1, D_BRANCH)),
                _resident((None, D_BRANCH, D_BRANCH), lambda n: (l, 0, 0))]
    args = [us, ur, x, x, mod4, w_out_b, final_g2, consts["cs"], wfold, consts["tpool"], pool_scale3,
            diff_lambda, subln3, consts["conv_fwd"], consts["conv_inv"], conv_spec, dwb3, lng3, lnb3, w_pw_b]
    if with_past:
        cache_spec = pl.BlockSpec((None, None, N_ATT_HEADS, PAST_LEN, HEAD_W),
                                  lambda n: (n // nrb, l, 0, 0, 0))
        in_specs += [cache_spec, cache_spec]
        args += [cache_k, cache_v]
    scratch = [pltpu.VMEM((CONV_PAD + s + CONV_TAIL, D_BRANCH), BF16),
               pltpu.VMEM((CONV_PAD + s + CONV_TAIL, D_BRANCH), BF16),
               pltpu.VMEM((N_ATT_HEADS, sk, HEAD_W), BF16),
               pltpu.VMEM((N_ATT_HEADS, sk, 2 * HEAD_W), BF16),
               pltpu.VMEM((ROW_BLOCK, D_MODEL), BF16),
               pltpu.VMEM((ROW_BLOCK, D_MODEL), BF16)]
    return pl.pallas_call(
        _make_mixer_kernel(s, l, with_past, final),
        out_shape=jax.ShapeDtypeStruct(x.shape, F32),
        grid=(n_blocks,),
        in_specs=in_specs,
        out_specs=pl.BlockSpec((2 * ROW_BLOCK, D_MODEL), lambda n: (prev(n) // 2, 0)),
        scratch_shapes=scratch,
        compiler_params=pltpu.CompilerParams(dimension_semantics=("arbitrary",), vmem_limit_bytes=VMEM_LIMIT),
        name="mixer_ctx" if prompt else "mixer_lat",
    )(*args)


def kernel(x_prompt, x_sample, cache_k, cache_v, c, c_ctx, norm_g, w_mod, b_mod, w_in, w_fourier, w_pool,
           pool_scale, diff_lambda, subln_g, conv_dw, conv_dw_b, conv_ln_g, conv_ln_b, w_conv_pw, w_out, final_g):
    w_in_b = w_in[0].astype(BF16)
    w_pw_b = w_conv_pw.astype(BF16)
    w_px = _fold_pool(w_in, w_pool)
    wfold = _fold_fourier(w_fourier)

    cv = jnp.concatenate([c_ctx[None, :], c, jnp.zeros((MOD_ROWS - 1 - DEC_BATCH, D_MODEL), F32)], axis=0)
    mod4 = _modulation(cv, w_mod, b_mod).reshape(DEPTH, MOD_ROWS, 1, 3 * D_MODEL)

    norm_g3 = norm_g.reshape(DEPTH, 1, D_MODEL)
    final_g2 = final_g.reshape(1, D_MODEL)
    params = (wfold, pool_scale.reshape(DEPTH, 1, D_BRANCH), diff_lambda,
              subln_g.reshape(DEPTH, 1, HEAD_W),
              _conv_spectrum(conv_dw),
              conv_dw_b.reshape(DEPTH, 1, D_BRANCH),
              conv_ln_g.reshape(DEPTH, 1, D_BRANCH), conv_ln_b.reshape(DEPTH, 1, D_BRANCH), w_pw_b)

    tpool = jnp.asarray(_pool_windows()).astype(BF16)
    conv_fwd, conv_inv = (jnp.asarray(m).astype(BF16) for m in _conv_dft())
    shared = {"tpool": tpool, "conv_fwd": conv_fwd, "conv_inv": conv_inv}
    consts_ctx = dict(shared, cs=jnp.asarray(_dft_positions(SEQ)).astype(BF16))
    consts_lat = dict(shared, cs=jnp.asarray(_dft_positions(DEC_SEQ)).astype(BF16))
    rope_tables = tuple(jnp.asarray(t) for t in _rope_tables(DEC_SEQ))

    xp = x_prompt.reshape(BATCH * SEQ, D_MODEL)
    xs = x_sample.reshape(DEC_BATCH * DEC_SEQ, D_MODEL)
    caches = None
    for l in range(DEPTH):
        final = l == DEPTH - 1
        casts = ((w_out, l),) if final else ((w_out, l), (w_in, l + 1))
        us, ur, kc, vc, w_out_b, *next_w_in = _inproj(xp, mod4, norm_g3, w_in_b, w_px, l, prompt=True,
                                                      caches=caches, casts=casts)
        caches = (kc, vc)
        xp = _mixer(us, ur, xp, mod4, w_out_b, final_g2, consts_ctx, params, l, SEQ, prompt=True, final=final)

        us, ur = _inproj(xs, mod4, norm_g3, w_in_b, w_px, l, prompt=False, rope_tables=rope_tables)
        xs = _mixer(us, ur, xs, mod4, w_out_b, final_g2, consts_lat, params, l, DEC_SEQ, prompt=False,
                    final=final, cache_k=cache_k, cache_v=cache_v)
        if next_w_in:
            (w_in_b,) = next_w_in

    return (xp.reshape(BATCH, SEQ, D_MODEL), xs.reshape(DEC_BATCH, DEC_SEQ, D_MODEL), caches[0], caches[1])
```

```python
import functools
import math

import numpy as np
import jax
import jax.numpy as jnp
from jax import lax
from jax.experimental import pallas as pl
from jax.experimental.pallas import tpu as pltpu

D_MODEL = 2048
BATCH = 32
SEQ = 256
DEPTH = 2
DEC_BATCH = 8
DEC_SEQ = 1024
PAST_LEN = 256
GRID_W = 64
D_BRANCH = 512
FFT_GROUP = 128
POOL_WINDOWS = (2, 4, 8, 16)
POOL_GROUP = 128
N_ATT_HEADS = 4
DIFF_HEAD = 64
HEAD_W = 128
N_FREQ = 16
CONV_WIDTH = 31
ROPE_BASE = 10000.0
EPS = 1e-6
N_IN_CHUNKS = 11
D_IN_PROJ = N_IN_CHUNKS * D_BRANCH
MOD_ROWS = 16

W_FX, W_FG, W_PX, W_PG, W_Q, W_K, W_V, W_AG, W_CA, W_CB, W_CG = range(N_IN_CHUNKS)
D_ACT = 5 * D_BRANCH
S_FX, S_PX, S_K, S_V, S_GLU = (i * D_BRANCH for i in range(5))
R_FG, R_PG, R_Q, R_AG, R_CG = (i * D_BRANCH for i in range(5))

Q_SCALE = DIFF_HEAD ** -0.5 * math.log2(math.e)

ROW_BLOCK = 256
STEP_BLOCKS = 2
CONV_PAD = 16
CONV_DFT = 256
CONV_OUT = 128
CONV_TAIL = CONV_DFT - CONV_OUT - CONV_PAD
CONV_TAPS_PAD = 32
TOKEN_TILE = 512
VMEM_LIMIT = 62 * 1024 * 1024

F32 = jnp.float32
BF16 = jnp.bfloat16


def _sigmoid(x):
    return 0.5 * jnp.tanh(0.5 * x) + 0.5


def _silu(x):
    return x * _sigmoid(x)


def _resident(block_shape, index_map):
    return pl.BlockSpec(block_shape, index_map, pipeline_mode=pl.Buffered(1))


@functools.lru_cache(maxsize=None)
def _dft_positions(s):
    k = np.arange(s, dtype=np.int64)
    ang = 2.0 * np.pi * ((k[:, None] * k[None, :]) % s) / s
    return (np.concatenate([np.cos(ang), np.sin(ang)], axis=0) / np.sqrt(s)).astype(np.float32)


@functools.lru_cache(maxsize=None)
def _dft_channels():
    k = np.arange(FFT_GROUP, dtype=np.int64)
    ang = 2.0 * np.pi * ((k[:, None] * k[None, :]) % FFT_GROUP) / FFT_GROUP
    return (np.stack([np.cos(ang), -np.sin(ang)]) / np.sqrt(FFT_GROUP)).astype(np.float32)


@functools.lru_cache(maxsize=None)
def _pool_windows():
    i = np.arange(CONV_OUT)[:, None]
    j = np.arange(CONV_DFT)[None, :] - CONV_PAD
    return np.stack([((j >= i - w // 2) & (j < i + w // 2)) for w in POOL_WINDOWS]).astype(np.float32)


@functools.lru_cache(maxsize=None)
def _rope_tables(s):
    rows = s // GRID_W
    t_row = np.repeat(np.arange(rows), GRID_W).astype(np.float64)
    t_col = np.tile(np.arange(GRID_W), rows).astype(np.float64)
    inv = ROPE_BASE ** (-np.arange(N_FREQ, dtype=np.float64) / N_FREQ)
    lane = np.arange(HEAD_W)
    d = lane % DIFF_HEAD
    axis = d // (2 * N_FREQ)
    half = (d % (2 * N_FREQ)) // N_FREQ
    freq = d % N_FREQ
    pos = np.where(axis[None, :] == 0, t_row[:, None], t_col[:, None])
    ang = pos * inv[freq][None, :]
    cos, sin = np.cos(ang), np.sin(ang)
    sin_a = np.where(half[None, :] == 0, -sin, 0.0)
    sin_b = np.where(half[None, :] == 1, sin, 0.0)
    return cos.astype(np.float32), sin_a.astype(np.float32), sin_b.astype(np.float32)


@functools.lru_cache(maxsize=None)
def _conv_dft():
    n, half = CONV_DFT, CONV_DFT // 2
    f = np.arange(half, dtype=np.int64)[:, None]
    u = np.arange(n, dtype=np.int64)[None, :]
    ang = 2.0 * np.pi * ((f * u) % n) / n
    fwd = np.concatenate([np.cos(ang), -np.sin(ang)], axis=0)
    fwd[half, :] = 1.0 - 2.0 * (u[0] % 2)
    t = np.arange(CONV_OUT, dtype=np.int64)[:, None]
    ang = 2.0 * np.pi * ((t * f.T) % n) / n
    inv_re = 2.0 * np.cos(ang) / n
    inv_re[:, 0] = 1.0 / n
    inv_im = -2.0 * np.sin(ang) / n
    inv_im[:, 0] = (1.0 - 2.0 * (t[:, 0] % 2)) / n
    return fwd.astype(np.float32), np.concatenate([inv_re, inv_im], axis=1).astype(np.float32)


@functools.lru_cache(maxsize=None)
def _conv_spectrum_trig():
    n, half = CONV_DFT, CONV_DFT // 2
    f = np.arange(half, dtype=np.int64)[:, None]
    lag = (np.arange(CONV_TAPS_PAD, dtype=np.int64) + CONV_PAD - CONV_WIDTH // 2)[None, :]
    ang = 2.0 * np.pi * ((f * lag) % n) / n
    cos, sin = np.cos(ang), np.sin(ang)
    cos_nyq = cos.copy()
    cos_nyq[0, :] = 1.0 - 2.0 * (lag[0] % 2)
    return np.stack([cos, sin, cos_nyq]).astype(np.float32)


def _spectrum_kernel(trig_ref, dw_ref, o_ref):
    for k in range(3):
        o_ref[k] = jnp.dot(trig_ref[k], dw_ref[...], preferred_element_type=F32, precision=lax.Precision.HIGHEST)


def _conv_spectrum(conv_dw):
    half = CONV_DFT // 2
    dw_pad = jnp.pad(conv_dw, ((0, 0), (0, CONV_TAPS_PAD - CONV_WIDTH), (0, 0)))
    return pl.pallas_call(
        _spectrum_kernel,
        out_shape=jax.ShapeDtypeStruct((DEPTH, 3, half, D_BRANCH), F32),
        grid=(DEPTH,),
        in_specs=[pl.BlockSpec((3, half, CONV_TAPS_PAD), lambda l: (0, 0, 0)),
                  pl.BlockSpec((None, CONV_TAPS_PAD, D_BRANCH), lambda l: (l, 0, 0))],
        out_specs=pl.BlockSpec((None, 3, half, D_BRANCH), lambda l: (l, 0, 0, 0)),
        name="conv_spectrum",
    )(jnp.asarray(_conv_spectrum_trig()), dw_pad)


def _fold_kernel(dft_ref, wf_ref, o_ref):
    for part in range(2):
        for g in range(D_BRANCH // FFT_GROUP):
            rows = slice(g * FFT_GROUP, (g + 1) * FFT_GROUP)
            prod = jnp.dot(dft_ref[part], wf_ref[rows, :], preferred_element_type=F32,
                           precision=lax.Precision.HIGHEST)
            o_ref[part * D_BRANCH + g * FFT_GROUP:part * D_BRANCH + (g + 1) * FFT_GROUP, :] = prod.astype(BF16)


def _fold_fourier(w_fourier):
    dft = jnp.asarray(_dft_channels())
    return pl.pallas_call(
        _fold_kernel,
        out_shape=jax.ShapeDtypeStruct((DEPTH, 2 * D_BRANCH, D_BRANCH), BF16),
        grid=(DEPTH,),
        in_specs=[pl.BlockSpec((2, FFT_GROUP, FFT_GROUP), lambda l: (0, 0, 0)),
                  pl.BlockSpec((None, D_BRANCH, D_BRANCH), lambda l: (l, 0, 0))],
        out_specs=pl.BlockSpec((None, 2 * D_BRANCH, D_BRANCH), lambda l: (l, 0, 0)),
        name="fold_fourier",
    )(dft, w_fourier)


def _fold_pool_kernel(wpx_ref, wp_ref, o_ref):
    for g in range(len(POOL_WINDOWS)):
        gc = slice(g * POOL_GROUP, (g + 1) * POOL_GROUP)
        o_ref[:, gc] = jnp.dot(wpx_ref[:, gc].astype(BF16), wp_ref[g].astype(BF16),
                               preferred_element_type=F32).astype(BF16)


def _fold_pool(w_in, w_pool):
    return pl.pallas_call(
        _fold_pool_kernel,
        out_shape=jax.ShapeDtypeStruct((DEPTH, D_MODEL, D_BRANCH), BF16),
        grid=(DEPTH,),
        in_specs=[pl.BlockSpec((None, D_MODEL, D_BRANCH), lambda l: (l, 0, W_PX)),
                  pl.BlockSpec((None, len(POOL_WINDOWS), POOL_GROUP, POOL_GROUP), lambda l: (l, 0, 0, 0))],
        out_specs=pl.BlockSpec((None, D_MODEL, D_BRANCH), lambda l: (l, 0, 0)),
        name="fold_pool",
    )(w_in, w_pool)


def _mod_kernel(cv_ref, w_ref, b_ref, o_ref):
    a = _silu(cv_ref[...]).astype(BF16)
    o_ref[...] = jnp.dot(a, w_ref[...].astype(BF16), preferred_element_type=F32) + b_ref[...]


def _modulation(cv, w_mod, b_mod):
    tn = 1024
    return pl.pallas_call(
        _mod_kernel,
        out_shape=jax.ShapeDtypeStruct((DEPTH, MOD_ROWS, 3 * D_MODEL), F32),
        grid=(DEPTH, 3 * D_MODEL // tn),
        in_specs=[pl.BlockSpec((MOD_ROWS, D_MODEL), lambda l, j: (0, 0)),
                  pl.BlockSpec((None, D_MODEL, tn), lambda l, j: (l, 0, j)),
                  pl.BlockSpec((None, 1, tn), lambda l, j: (l, 0, j))],
        out_specs=pl.BlockSpec((None, MOD_ROWS, tn), lambda l, j: (l, 0, j)),
        compiler_params=pltpu.CompilerParams(vmem_limit_bytes=VMEM_LIMIT),
        name="modulation",
    )(cv, w_mod, b_mod.reshape(DEPTH, 1, 3 * D_MODEL))


def _rope(x, cos, sin_a, sin_b):
    return x * cos + pltpu.roll(x, HEAD_W - N_FREQ, 1) * sin_a + pltpu.roll(x, N_FREQ, 1) * sin_b


def _inproj_kernel(*refs, prompt, aliased_caches, n_cast):
    it = iter(refs)
    x_ref, mod_ref, g_ref, w_ref, wpx_ref = (next(it) for _ in range(5))
    cast_src = [next(it) for _ in range(n_cast)]
    if prompt:
        if aliased_caches:
            next(it), next(it)
        us_ref, ur_ref, kc_ref, vc_ref = (next(it) for _ in range(4))
    else:
        cos_ref, sa_ref, sb_ref, us_ref, ur_ref = (next(it) for _ in range(5))
    cast_dst = [next(it) for _ in range(n_cast)]
    h_ref = next(it)

    for src, dst in zip(cast_src, cast_dst):
        dst[...] = src[...].astype(BF16)

    tm = x_ref.shape[0]
    shift = mod_ref[:, 0:D_MODEL]
    scale = mod_ref[:, D_MODEL:2 * D_MODEL]
    for r in range(tm // 128):
        rows = slice(r * 128, (r + 1) * 128)
        x = x_ref[rows, :]
        y = x * lax.rsqrt(jnp.mean(x * x, axis=-1, keepdims=True) + EPS) * g_ref[...]
        h_ref[rows, :] = (y * (1.0 + scale) + shift).astype(BF16)

    def proj(c):
        return jnp.dot(h_ref[...], w_ref[:, c * D_BRANCH:(c + 1) * D_BRANCH], preferred_element_type=F32)

    def put(dst, col, val):
        dst[:, col:col + val.shape[1]] = val.astype(BF16)

    def put_cache(dst, val):
        for bb in range(tm // SEQ):
            for h in range(N_ATT_HEADS):
                dst[bb, h] = val[bb * SEQ:(bb + 1) * SEQ, h * HEAD_W:(h + 1) * HEAD_W]

    put(us_ref, S_FX, proj(W_FX))
    put(ur_ref, R_FG, _silu(proj(W_FG)))
    put(us_ref, S_PX, jnp.dot(h_ref[...], wpx_ref[...], preferred_element_type=F32))
    put(ur_ref, R_PG, _silu(proj(W_PG)))
    q = proj(W_Q)
    k = proj(W_K)
    if prompt:
        put_cache(kc_ref, k)
        put(ur_ref, R_Q, q * Q_SCALE)
        put(us_ref, S_K, k)
    else:
        cos, sa, sb = cos_ref[...], sa_ref[...], sb_ref[...]
        for h in range(N_ATT_HEADS):
            hc = slice(h * HEAD_W, (h + 1) * HEAD_W)
            put(ur_ref, R_Q + h * HEAD_W, _rope(q[:, hc], cos, sa, sb) * Q_SCALE)
            put(us_ref, S_K + h * HEAD_W, _rope(k[:, hc], cos, sa, sb))
    v = proj(W_V)
    if prompt:
        put_cache(vc_ref, v)
    put(us_ref, S_V, v)
    put(ur_ref, R_AG, _silu(proj(W_AG)))
    put(us_ref, S_GLU, proj(W_CA) * _sigmoid(proj(W_CB)))
    put(ur_ref, R_CG, _silu(proj(W_CG)))


def _inproj(x, mod4, norm_g3, w_in_b, w_px, l, prompt, rope_tables=None, caches=None, casts=()):
    t = x.shape[0]
    tm = TOKEN_TILE
    steps = t // tm
    if prompt:
        row = lambda i: 0
    else:
        row = lambda i: 1 + (i * tm) // DEC_SEQ
    in_specs = [pl.BlockSpec((tm, D_MODEL), lambda i: (i, 0)),
                pl.BlockSpec((None, None, 1, 3 * D_MODEL), lambda i: (l, row(i), 0, 0)),
                pl.BlockSpec((None, 1, D_MODEL), lambda i: (l, 0, 0)),
                _resident((D_MODEL, D_IN_PROJ), lambda i: (0, 0)),
                _resident((None, D_MODEL, D_BRANCH), lambda i: (l, 0, 0))]
    args = [x, mod4, norm_g3, w_in_b, w_px]
    in_specs += [pl.BlockSpec((None, w.shape[1] // steps, w.shape[2]), lambda i, wl=wl: (wl, i, 0))
                 for w, wl in casts]
    args += [w for w, _ in casts]
    act = jax.ShapeDtypeStruct((t, D_ACT), BF16)
    act_spec = pl.BlockSpec((tm, D_ACT), lambda i: (i, 0))
    out_shape = [act, act]
    out_specs = [act_spec, act_spec]
    aliases = {}
    if prompt:
        cache = jax.ShapeDtypeStruct((BATCH, DEPTH, N_ATT_HEADS, SEQ, HEAD_W), F32)
        cache_spec = pl.BlockSpec((tm // SEQ, None, N_ATT_HEADS, SEQ, HEAD_W), lambda i: (i, l, 0, 0, 0))
        out_shape += [cache, cache]
        out_specs += [cache_spec, cache_spec]
        if caches is not None:
            aliases = {len(args): 2, len(args) + 1: 3}
            in_specs += [pl.BlockSpec(memory_space=pl.ANY)] * 2
            args += list(caches)
    else:
        nblk = DEC_SEQ // tm
        in_specs += [pl.BlockSpec((tm, HEAD_W), lambda i: (i % nblk, 0))] * 3
        args += list(rope_tables)
    out_shape += [jax.ShapeDtypeStruct(w.shape[1:], BF16) for w, _ in casts]
    out_specs += [pl.BlockSpec((w.shape[1] // steps, w.shape[2]), lambda i: (i, 0)) for w, _ in casts]
    return pl.pallas_call(
        functools.partial(_inproj_kernel, prompt=prompt, aliased_caches=caches is not None,
                          n_cast=len(casts)),
        out_shape=out_shape,
        grid=(steps,),
        in_specs=in_specs,
        out_specs=out_specs,
        scratch_shapes=[pltpu.VMEM((tm, D_MODEL), BF16)],
        input_output_aliases=aliases,
        compiler_params=pltpu.CompilerParams(dimension_semantics=("arbitrary",), vmem_limit_bytes=VMEM_LIMIT),
        name="inproj_ctx" if prompt else "inproj_lat",
    )(*args)


def _make_mixer_kernel(s, l, with_past, final):
    lam_init = 0.8 - 0.6 * math.exp(-0.3 * l)
    past = PAST_LEN if with_past else 0
    sk = s + past
    nrb = s // ROW_BLOCK
    rb_rows = ROW_BLOCK
    n_out_chunks = D_MODEL // D_BRANCH
    nseq = max(STEP_BLOCKS // nrb, 1)
    steps_per_seq = max(nrb // STEP_BLOCKS, 1)

    def kern(*refs):
        it = iter(refs)
        us_ref, ur_ref, x_ref, mod_ref, wo_ref, fg_ref = (next(it) for _ in range(6))
        cs_ref, wf_ref, tp_ref, ps_ref = (next(it) for _ in range(4))
        dl_ref, sg_ref, fwd_ref, inv_ref, spec_ref, dwb_ref, lng_ref, lnb_ref, wpw_ref = (
            next(it) for _ in range(9))
        if with_past:
            ck_ref, cv_ref = next(it), next(it)
        o_ref = next(it)
        ppad_ref, gpad_ref, kall_ref, vx_ref, z_ref = (next(it) for _ in range(5))

        n = pl.program_id(0)

        def cols(c0, g=0, w=D_BRANCH):
            return slice(c0 + g * w, c0 + (g + 1) * w)

        def rows_at(base, size):
            return pl.ds(base, size) if nrb > 1 else slice(base, base + size)

        def stage_sequences():
            ones = jnp.ones((sk, HEAD_W), BF16)
            for q in range(nseq):
                seq_rows = slice(q * s, (q + 1) * s)
                for pad_ref, col0 in ((ppad_ref, S_PX), (gpad_ref, S_GLU)):
                    pad_ref[q, 0:CONV_PAD, :] = jnp.zeros((CONV_PAD, D_BRANCH), BF16)
                    pad_ref[q, CONV_PAD + s:CONV_PAD + s + CONV_TAIL, :] = jnp.zeros((CONV_TAIL, D_BRANCH), BF16)
                    pad_ref[q, CONV_PAD:CONV_PAD + s, :] = us_ref[seq_rows, cols(col0)]
                for h in range(N_ATT_HEADS):
                    if with_past:
                        kall_ref[q, h, 0:past, :] = ck_ref[h].astype(BF16)
                        vx_ref[q, h, 0:past, 0:HEAD_W] = cv_ref[h].astype(BF16)
                    kall_ref[q, h, past:sk, :] = us_ref[seq_rows, cols(S_K, h, HEAD_W)]
                    vx_ref[q, h, past:sk, 0:HEAD_W] = us_ref[seq_rows, cols(S_V, h, HEAD_W)]
                    vx_ref[q, h, :, HEAD_W:2 * HEAD_W] = ones

        if steps_per_seq == 1:
            stage_sequences()
        else:
            pl.when(n % steps_per_seq == 0)(stage_sequences)

        def block_place(j):
            if nrb == 1:
                return j, 0, slice(j * rb_rows, (j + 1) * rb_rows)
            r0 = pl.multiple_of(((n % steps_per_seq) * STEP_BLOCKS + j) * rb_rows, rb_rows)
            return 0, r0, slice(j * rb_rows, (j + 1) * rb_rows)

        def out_projection():
            step_rows = STEP_BLOCKS * rb_rows
            for c in range(n_out_chunks):
                y = jnp.dot(z_ref[...], wo_ref[:, cols(0, c)], preferred_element_type=F32)
                gate = mod_ref[:, 2 * D_MODEL + c * D_BRANCH:2 * D_MODEL + (c + 1) * D_BRANCH]
                o_ref[:, cols(0, c)] = x_ref[:, cols(0, c)] + gate * y
            if final:
                ssq = jnp.zeros((step_rows, 1), F32)
                for c in range(n_out_chunks):
                    out = o_ref[:, cols(0, c)]
                    ssq = ssq + jnp.sum(out * out, axis=-1, keepdims=True)
                inv = lax.rsqrt(ssq * (1.0 / D_MODEL) + EPS)
                for c in range(n_out_chunks):
                    o_ref[:, cols(0, c)] = o_ref[:, cols(0, c)] * inv * fg_ref[:, cols(0, c)]

        def fourier(j):
            q, r0, rows = block_place(j)
            fx = us_ref[q * s:(q + 1) * s, cols(S_FX)]
            pr = jnp.dot(cs_ref[rows_at(r0, rb_rows), :], fx, preferred_element_type=F32)
            pi = jnp.dot(cs_ref[rows_at(s + r0, rb_rows), :], fx, preferred_element_type=F32)
            yf = (jnp.dot(pr.astype(BF16), wf_ref[0:D_BRANCH, :], preferred_element_type=F32)
                  + jnp.dot(pi.astype(BF16), wf_ref[D_BRANCH:2 * D_BRANCH, :], preferred_element_type=F32))
            z_ref[rows, cols(0)] = (yf * ur_ref[rows, cols(R_FG)].astype(F32)).astype(BF16)

        def pooling(j):
            q, r0, rows = block_place(j)
            for k in range(rb_rows // CONV_OUT):
                base = r0 + k * CONV_OUT
                t = base + lax.broadcasted_iota(jnp.int32, (CONV_OUT, POOL_GROUP), 0)
                out_rows = slice(rows.start + k * CONV_OUT, rows.start + (k + 1) * CONV_OUT)
                for g, w in enumerate(POOL_WINDOWS):
                    win = ppad_ref[q, rows_at(base, CONV_DFT), cols(0, g, POOL_GROUP)]
                    ssum = jnp.dot(tp_ref[g], win, preferred_element_type=F32)
                    cnt = (jnp.minimum(t + w // 2, s) - jnp.maximum(t - w // 2, 0)).astype(F32)
                    px = ppad_ref[q, rows_at(base + CONV_PAD, CONV_OUT), cols(0, g, POOL_GROUP)].astype(F32)
                    pg = ur_ref[out_rows, cols(R_PG, g, POOL_GROUP)].astype(F32)
                    z_ref[out_rows, cols(D_BRANCH, g, POOL_GROUP)] = (
                        (ssum / cnt - px) * ps_ref[:, cols(0, g, POOL_GROUP)] * pg).astype(BF16)

        dl = dl_ref[...]
        lam = (jnp.exp(jnp.sum(dl[0:1] * dl[1:2], axis=-1, keepdims=True))
               - jnp.exp(jnp.sum(dl[2:3] * dl[3:4], axis=-1, keepdims=True)) + lam_init)
        first_map = lax.broadcasted_iota(jnp.int32, (rb_rows, HEAD_W), 1) < DIFF_HEAD

        def attention(h, j):
            q, _, rows = block_place(j)
            qf = ur_ref[rows, cols(R_Q, h, HEAD_W)].astype(F32)
            qq = jnp.concatenate([jnp.where(first_map, qf, 0.0).astype(BF16),
                                  jnp.where(first_map, 0.0, qf).astype(BF16)], axis=0)
            sc = lax.dot_general(qq, kall_ref[q, h], (((1,), (1,)), ((), ())), preferred_element_type=F32)
            m = jnp.max(sc, axis=-1, keepdims=True)
            p = jnp.exp2((sc - m).astype(BF16))
            ov = jnp.dot(p, vx_ref[q, h], preferred_element_type=F32)
            o = (ov[0:rb_rows, 0:HEAD_W] / ov[0:rb_rows, HEAD_W:2 * HEAD_W]
                 - lam * (ov[rb_rows:2 * rb_rows, 0:HEAD_W] / ov[rb_rows:2 * rb_rows, HEAD_W:2 * HEAD_W]))
            o = o * lax.rsqrt(jnp.mean(o * o, axis=-1, keepdims=True) + EPS) * sg_ref[...] * (1.0 - lam_init)
            ag = ur_ref[rows, cols(R_AG, h, HEAD_W)].astype(F32)
            z_ref[rows, cols(2 * D_BRANCH, h, HEAD_W)] = (o * ag).astype(BF16)

        def conv_module(j):
            q, r0, rows = block_place(j)
            half = CONV_DFT // 2
            parts = []
            for k in range(rb_rows // CONV_OUT):
                win = gpad_ref[q, rows_at(r0 + k * CONV_OUT, CONV_DFT), :]
                spec = jnp.dot(fwd_ref[...], win, preferred_element_type=F32)
                yr, yi = spec[0:half], spec[half:CONV_DFT]
                zr = yr * spec_ref[0] - yi * spec_ref[1]
                zi = yr * spec_ref[1] + yi * spec_ref[2]
                z = jnp.concatenate([zr, zi], axis=0).astype(BF16)
                parts.append(jnp.dot(inv_ref[...], z, preferred_element_type=F32))
            y = jnp.concatenate(parts, axis=0) + dwb_ref[...]
            mu = jnp.mean(y, axis=-1, keepdims=True)
            d = y - mu
            var = jnp.mean(d * d, axis=-1, keepdims=True)
            act = _silu(d * lax.rsqrt(var + EPS) * lng_ref[...] + lnb_ref[...]).astype(BF16)
            yc = jnp.dot(act, wpw_ref[...], preferred_element_type=F32)
            z_ref[rows, cols(3 * D_BRANCH)] = (yc * ur_ref[rows, cols(R_CG)].astype(F32)).astype(BF16)

        pieces = [conv_module, fourier, pooling] + [functools.partial(attention, h) for h in range(N_ATT_HEADS)]
        for piece in pieces:
            for j in range(STEP_BLOCKS):
                piece(j)
        out_projection()

    return kern


def _mixer(us, ur, x, mod4, w_out_b, final_g2, consts, params, l, s, prompt, final, cache_k=None, cache_v=None):
    with_past = cache_k is not None
    nrb = s // ROW_BLOCK
    n_blocks = x.shape[0] // ROW_BLOCK
    sk = s + (PAST_LEN if with_past else 0)
    wfold, pool_scale3, diff_lambda, subln3, conv_spec, dwb3, lng3, lnb3, w_pw_b = params

    step_rows = STEP_BLOCKS * ROW_BLOCK
    nseq = max(STEP_BLOCKS // nrb, 1)
    steps_per_seq = max(nrb // STEP_BLOCKS, 1)
    assert n_blocks % STEP_BLOCKS == 0 and (nrb % STEP_BLOCKS == 0 or (nrb == 1 and prompt))

    def mod_row(n):
        return 0 if prompt else 1 + n // steps_per_seq

    def per_layer(shape):
        nd = len(shape)
        return pl.BlockSpec((None,) + shape, lambda n: (l,) + (0,) * nd)

    in_specs = [pl.BlockSpec((nseq * s, D_ACT), lambda n: (n // steps_per_seq, 0)),
                pl.BlockSpec((step_rows, D_ACT), lambda n: (n, 0)),
                pl.BlockSpec((step_rows, D_MODEL), lambda n: (n, 0)),
                pl.BlockSpec((None, None, 1, 3 * D_MODEL), lambda n: (l, mod_row(n), 0, 0)),
                _resident((D_MODEL, D_MODEL), lambda n: (0, 0)),
                pl.BlockSpec((1, D_MODEL), lambda n: (0, 0)),
                _resident((2 * s, s), lambda n: (0, 0)),
                _resident((None, 2 * D_BRANCH, D_BRANCH), lambda n: (l, 0, 0)),
                _resident((len(POOL_WINDOWS), CONV_OUT, CONV_DFT), lambda n: (0, 0, 0)),
                per_layer((1, D_BRANCH)),
                per_layer((4, DIFF_HEAD)),
                per_layer((1, HEAD_W)),
                _resident((CONV_DFT, CONV_DFT), lambda n: (0, 0)),
                _resident((CONV_OUT, CONV_DFT), lambda n: (0, 0)),
                _resident((None, 3, CONV_DFT // 2, D_BRANCH), lambda n: (l, 0, 0, 0)),
                per_layer((1, D_BRANCH)),
                per_layer((1, D_BRANCH)),
                per_layer((1, D_BRANCH)),
                _resident((None, D_BRANCH, D_BRANCH), lambda n: (l, 0, 0))]
    args = [us, ur, x, mod4, w_out_b, final_g2, consts["cs"], wfold, consts["tpool"], pool_scale3,
            diff_lambda, subln3, consts["conv_fwd"], consts["conv_inv"], conv_spec, dwb3, lng3, lnb3, w_pw_b]
    if with_past:
        cache_spec = pl.BlockSpec((None, None, N_ATT_HEADS, PAST_LEN, HEAD_W),
                                  lambda n: (n // steps_per_seq, l, 0, 0, 0))
        in_specs += [cache_spec, cache_spec]
        args += [cache_k, cache_v]
    scratch = [pltpu.VMEM((nseq, CONV_PAD + s + CONV_TAIL, D_BRANCH), BF16),
               pltpu.VMEM((nseq, CONV_PAD + s + CONV_TAIL, D_BRANCH), BF16),
               pltpu.VMEM((nseq, N_ATT_HEADS, sk, HEAD_W), BF16),
               pltpu.VMEM((nseq, N_ATT_HEADS, sk, 2 * HEAD_W), BF16),
               pltpu.VMEM((step_rows, D_MODEL), BF16)]
    return pl.pallas_call(
        _make_mixer_kernel(s, l, with_past, final),
        out_shape=jax.ShapeDtypeStruct(x.shape, F32),
        grid=(n_blocks // STEP_BLOCKS,),
        in_specs=in_specs,
        out_specs=pl.BlockSpec((step_rows, D_MODEL), lambda n: (n, 0)),
        scratch_shapes=scratch,
        compiler_params=pltpu.CompilerParams(dimension_semantics=("arbitrary",), vmem_limit_bytes=VMEM_LIMIT),
        name="mixer_ctx" if prompt else "mixer_lat",
    )(*args)


def kernel(x_prompt, x_sample, cache_k, cache_v, c, c_ctx, norm_g, w_mod, b_mod, w_in, w_fourier, w_pool,
           pool_scale, diff_lambda, subln_g, conv_dw, conv_dw_b, conv_ln_g, conv_ln_b, w_conv_pw, w_out, final_g):
    w_in_b = w_in[0].astype(BF16)
    w_pw_b = w_conv_pw.astype(BF16)
    w_px = _fold_pool(w_in, w_pool)
    wfold = _fold_fourier(w_fourier)

    cv = jnp.concatenate([c_ctx[None, :], c, jnp.zeros((MOD_ROWS - 1 - DEC_BATCH, D_MODEL), F32)], axis=0)
    mod4 = _modulation(cv, w_mod, b_mod).reshape(DEPTH, MOD_ROWS, 1, 3 * D_MODEL)

    norm_g3 = norm_g.reshape(DEPTH, 1, D_MODEL)
    final_g2 = final_g.reshape(1, D_MODEL)
    params = (wfold, pool_scale.reshape(DEPTH, 1, D_BRANCH), diff_lambda,
              subln_g.reshape(DEPTH, 1, HEAD_W),
              _conv_spectrum(conv_dw),
              conv_dw_b.reshape(DEPTH, 1, D_BRANCH),
              conv_ln_g.reshape(DEPTH, 1, D_BRANCH), conv_ln_b.reshape(DEPTH, 1, D_BRANCH), w_pw_b)

    tpool = jnp.asarray(_pool_windows()).astype(BF16)
    conv_fwd, conv_inv = (jnp.asarray(m).astype(BF16) for m in _conv_dft())
    shared = {"tpool": tpool, "conv_fwd": conv_fwd, "conv_inv": conv_inv}
    consts_ctx = dict(shared, cs=jnp.asarray(_dft_positions(SEQ)).astype(BF16))
    consts_lat = dict(shared, cs=jnp.asarray(_dft_positions(DEC_SEQ)).astype(BF16))
    rope_tables = tuple(jnp.asarray(t) for t in _rope_tables(DEC_SEQ))

    xp = x_prompt.reshape(BATCH * SEQ, D_MODEL)
    xs = x_sample.reshape(DEC_BATCH * DEC_SEQ, D_MODEL)
    caches = None
    for l in range(DEPTH):
        final = l == DEPTH - 1
        casts = ((w_out, l),) if final else ((w_out, l), (w_in, l + 1))
        us, ur, kc, vc, w_out_b, *next_w_in = _inproj(xp, mod4, norm_g3, w_in_b, w_px, l, prompt=True,
                                                      caches=caches, casts=casts)
        caches = (kc, vc)
        xp = _mixer(us, ur, xp, mod4, w_out_b, final_g2, consts_ctx, params, l, SEQ, prompt=True, final=final)

        us, ur = _inproj(xs, mod4, norm_g3, w_in_b, w_px, l, prompt=False, rope_tables=rope_tables)
        xs = _mixer(us, ur, xs, mod4, w_out_b, final_g2, consts_lat, params, l, DEC_SEQ, prompt=False,
                    final=final, cache_k=cache_k, cache_v=cache_v)
        if next_w_in:
            (w_in_b,) = next_w_in

    return (xp.reshape(BATCH, SEQ, D_MODEL), xs.reshape(DEC_BATCH, DEC_SEQ, D_MODEL), caches[0], caches[1])
```

```python
import functools
import math

import numpy as np
import jax
import jax.numpy as jnp
from jax import lax
from jax.experimental import pallas as pl
from jax.experimental.pallas import tpu as pltpu

D_MODEL = 2048
BATCH = 32
SEQ = 256
DEPTH = 2
DEC_BATCH = 8
DEC_SEQ = 1024
PAST_LEN = 256
GRID_W = 64
D_BRANCH = 512
FFT_GROUP = 128
POOL_WINDOWS = (2, 4, 8, 16)
POOL_GROUP = 128
N_ATT_HEADS = 4
DIFF_HEAD = 64
HEAD_W = 128
N_FREQ = 16
CONV_WIDTH = 31
ROPE_BASE = 10000.0
EPS = 1e-6
N_IN_CHUNKS = 11
D_IN_PROJ = N_IN_CHUNKS * D_BRANCH
MOD_ROWS = 16

W_FX, W_FG, W_PX, W_PG, W_Q, W_K, W_V, W_AG, W_CA, W_CB, W_CG = range(N_IN_CHUNKS)
D_ACT = 5 * D_BRANCH
S_FX, S_PX, S_K, S_V, S_GLU = (i * D_BRANCH for i in range(5))
R_FG, R_PG, R_Q, R_AG, R_CG = (i * D_BRANCH for i in range(5))

Q_SCALE = DIFF_HEAD ** -0.5 * math.log2(math.e)

ROW_BLOCK = 256
STEP_BLOCKS = 2
CONV_PAD = 16
CONV_DFT = 256
CONV_OUT = 128
CONV_TAIL = CONV_DFT - CONV_OUT - CONV_PAD
CONV_TAPS_PAD = 32
TOKEN_TILE = 512
VMEM_LIMIT = 62 * 1024 * 1024

F32 = jnp.float32
BF16 = jnp.bfloat16


def _sigmoid(x):
    return 0.5 * jnp.tanh(0.5 * x) + 0.5


def _silu(x):
    return x * _sigmoid(x)


def _resident(block_shape, index_map):
    return pl.BlockSpec(block_shape, index_map, pipeline_mode=pl.Buffered(1))


@functools.lru_cache(maxsize=None)
def _dft_positions(s):
    k = np.arange(s, dtype=np.int64)
    ang = 2.0 * np.pi * ((k[:, None] * k[None, :]) % s) / s
    return (np.concatenate([np.cos(ang), np.sin(ang)], axis=0) / np.sqrt(s)).astype(np.float32)


@functools.lru_cache(maxsize=None)
def _dft_channels():
    k = np.arange(FFT_GROUP, dtype=np.int64)
    ang = 2.0 * np.pi * ((k[:, None] * k[None, :]) % FFT_GROUP) / FFT_GROUP
    return (np.stack([np.cos(ang), -np.sin(ang)]) / np.sqrt(FFT_GROUP)).astype(np.float32)


@functools.lru_cache(maxsize=None)
def _pool_windows():
    i = np.arange(CONV_OUT)[:, None]
    j = np.arange(CONV_DFT)[None, :] - CONV_PAD
    return np.stack([((j >= i - w // 2) & (j < i + w // 2)) for w in POOL_WINDOWS]).astype(np.float32)


@functools.lru_cache(maxsize=None)
def _rope_tables(s):
    rows = s // GRID_W
    t_row = np.repeat(np.arange(rows), GRID_W).astype(np.float64)
    t_col = np.tile(np.arange(GRID_W), rows).astype(np.float64)
    inv = ROPE_BASE ** (-np.arange(N_FREQ, dtype=np.float64) / N_FREQ)
    lane = np.arange(HEAD_W)
    d = lane % DIFF_HEAD
    axis = d // (2 * N_FREQ)
    half = (d % (2 * N_FREQ)) // N_FREQ
    freq = d % N_FREQ
    pos = np.where(axis[None, :] == 0, t_row[:, None], t_col[:, None])
    ang = pos * inv[freq][None, :]
    cos, sin = np.cos(ang), np.sin(ang)
    sin_a = np.where(half[None, :] == 0, -sin, 0.0)
    sin_b = np.where(half[None, :] == 1, sin, 0.0)
    return cos.astype(np.float32), sin_a.astype(np.float32), sin_b.astype(np.float32)


@functools.lru_cache(maxsize=None)
def _conv_dft():
    n, half = CONV_DFT, CONV_DFT // 2
    f = np.arange(half, dtype=np.int64)[:, None]
    u = np.arange(n, dtype=np.int64)[None, :]
    ang = 2.0 * np.pi * ((f * u) % n) / n
    fwd = np.concatenate([np.cos(ang), -np.sin(ang)], axis=0)
    fwd[half, :] = 1.0 - 2.0 * (u[0] % 2)
    t = np.arange(CONV_OUT, dtype=np.int64)[:, None]
    ang = 2.0 * np.pi * ((t * f.T) % n) / n
    inv_re = 2.0 * np.cos(ang) / n
    inv_re[:, 0] = 1.0 / n
    inv_im = -2.0 * np.sin(ang) / n
    inv_im[:, 0] = (1.0 - 2.0 * (t[:, 0] % 2)) / n
    return fwd.astype(np.float32), np.concatenate([inv_re, inv_im], axis=1).astype(np.float32)


@functools.lru_cache(maxsize=None)
def _conv_spectrum_trig():
    n, half = CONV_DFT, CONV_DFT // 2
    f = np.arange(half, dtype=np.int64)[:, None]
    lag = (np.arange(CONV_TAPS_PAD, dtype=np.int64) + CONV_PAD - CONV_WIDTH // 2)[None, :]
    ang = 2.0 * np.pi * ((f * lag) % n) / n
    cos, sin = np.cos(ang), np.sin(ang)
    cos_nyq = cos.copy()
    cos_nyq[0, :] = 1.0 - 2.0 * (lag[0] % 2)
    return np.stack([cos, sin, cos_nyq]).astype(np.float32)


def _spectrum_kernel(trig_ref, dw_ref, o_ref):
    for k in range(3):
        o_ref[k] = jnp.dot(trig_ref[k], dw_ref[...], preferred_element_type=F32, precision=lax.Precision.HIGHEST)


def _conv_spectrum(conv_dw):
    half = CONV_DFT // 2
    dw_pad = jnp.pad(conv_dw, ((0, 0), (0, CONV_TAPS_PAD - CONV_WIDTH), (0, 0)))
    return pl.pallas_call(
        _spectrum_kernel,
        out_shape=jax.ShapeDtypeStruct((DEPTH, 3, half, D_BRANCH), F32),
        grid=(DEPTH,),
        in_specs=[pl.BlockSpec((3, half, CONV_TAPS_PAD), lambda l: (0, 0, 0)),
                  pl.BlockSpec((None, CONV_TAPS_PAD, D_BRANCH), lambda l: (l, 0, 0))],
        out_specs=pl.BlockSpec((None, 3, half, D_BRANCH), lambda l: (l, 0, 0, 0)),
        name="conv_spectrum",
    )(jnp.asarray(_conv_spectrum_trig()), dw_pad)


def _fold_kernel(dft_ref, wf_ref, o_ref):
    for part in range(2):
        for g in range(D_BRANCH // FFT_GROUP):
            rows = slice(g * FFT_GROUP, (g + 1) * FFT_GROUP)
            prod = jnp.dot(dft_ref[part], wf_ref[rows, :], preferred_element_type=F32,
                           precision=lax.Precision.HIGHEST)
            o_ref[part * D_BRANCH + g * FFT_GROUP:part * D_BRANCH + (g + 1) * FFT_GROUP, :] = prod.astype(BF16)


def _fold_fourier(w_fourier):
    dft = jnp.asarray(_dft_channels())
    return pl.pallas_call(
        _fold_kernel,
        out_shape=jax.ShapeDtypeStruct((DEPTH, 2 * D_BRANCH, D_BRANCH), BF16),
        grid=(DEPTH,),
        in_specs=[pl.BlockSpec((2, FFT_GROUP, FFT_GROUP), lambda l: (0, 0, 0)),
                  pl.BlockSpec((None, D_BRANCH, D_BRANCH), lambda l: (l, 0, 0))],
        out_specs=pl.BlockSpec((None, 2 * D_BRANCH, D_BRANCH), lambda l: (l, 0, 0)),
        name="fold_fourier",
    )(dft, w_fourier)


def _fold_pool_kernel(wpx_ref, wp_ref, o_ref):
    for g in range(len(POOL_WINDOWS)):
        gc = slice(g * POOL_GROUP, (g + 1) * POOL_GROUP)
        o_ref[:, gc] = jnp.dot(wpx_ref[:, gc].astype(BF16), wp_ref[g].astype(BF16),
                               preferred_element_type=F32).astype(BF16)


def _fold_pool(w_in, w_pool):
    return pl.pallas_call(
        _fold_pool_kernel,
        out_shape=jax.ShapeDtypeStruct((DEPTH, D_MODEL, D_BRANCH), BF16),
        grid=(DEPTH,),
        in_specs=[pl.BlockSpec((None, D_MODEL, D_BRANCH), lambda l: (l, 0, W_PX)),
                  pl.BlockSpec((None, len(POOL_WINDOWS), POOL_GROUP, POOL_GROUP), lambda l: (l, 0, 0, 0))],
        out_specs=pl.BlockSpec((None, D_MODEL, D_BRANCH), lambda l: (l, 0, 0)),
        name="fold_pool",
    )(w_in, w_pool)


def _mod_kernel(cv_ref, w_ref, b_ref, o_ref):
    a = _silu(cv_ref[...]).astype(BF16)
    o_ref[...] = jnp.dot(a, w_ref[...].astype(BF16), preferred_element_type=F32) + b_ref[...]


def _modulation(cv, w_mod, b_mod):
    tn = 1024
    return pl.pallas_call(
        _mod_kernel,
        out_shape=jax.ShapeDtypeStruct((DEPTH, MOD_ROWS, 3 * D_MODEL), F32),
        grid=(DEPTH, 3 * D_MODEL // tn),
        in_specs=[pl.BlockSpec((MOD_ROWS, D_MODEL), lambda l, j: (0, 0)),
                  pl.BlockSpec((None, D_MODEL, tn), lambda l, j: (l, 0, j)),
                  pl.BlockSpec((None, 1, tn), lambda l, j: (l, 0, j))],
        out_specs=pl.BlockSpec((None, MOD_ROWS, tn), lambda l, j: (l, 0, j)),
        compiler_params=pltpu.CompilerParams(vmem_limit_bytes=VMEM_LIMIT),
        name="modulation",
    )(cv, w_mod, b_mod.reshape(DEPTH, 1, 3 * D_MODEL))


def _rope(x, cos, sin_a, sin_b):
    return x * cos + pltpu.roll(x, HEAD_W - N_FREQ, 1) * sin_a + pltpu.roll(x, N_FREQ, 1) * sin_b


def _inproj_kernel(*refs, prompt, aliased_caches, n_cast):
    it = iter(refs)
    x_ref, mod_ref, g_ref, w_ref, wpx_ref = (next(it) for _ in range(5))
    cast_src = [next(it) for _ in range(n_cast)]
    if prompt:
        if aliased_caches:
            next(it), next(it)
        us_ref, ur_ref, kc_ref, vc_ref = (next(it) for _ in range(4))
    else:
        cos_ref, sa_ref, sb_ref, us_ref, ur_ref = (next(it) for _ in range(5))
    cast_dst = [next(it) for _ in range(n_cast)]
    h_ref = next(it)

    for src, dst in zip(cast_src, cast_dst):
        dst[...] = src[...].astype(BF16)

    tm = x_ref.shape[0]
    shift = mod_ref[:, 0:D_MODEL]
    scale = mod_ref[:, D_MODEL:2 * D_MODEL]
    for r in range(tm // 128):
        rows = slice(r * 128, (r + 1) * 128)
        x = x_ref[rows, :]
        y = x * lax.rsqrt(jnp.mean(x * x, axis=-1, keepdims=True) + EPS) * g_ref[...]
        h_ref[rows, :] = (y * (1.0 + scale) + shift).astype(BF16)

    def proj(c):
        return jnp.dot(h_ref[...], w_ref[:, c * D_BRANCH:(c + 1) * D_BRANCH], preferred_element_type=F32)

    def put(dst, col, val):
        dst[:, col:col + val.shape[1]] = val.astype(BF16)

    def put_cache(dst, val):
        for bb in range(tm // SEQ):
            for h in range(N_ATT_HEADS):
                dst[bb, h] = val[bb * SEQ:(bb + 1) * SEQ, h * HEAD_W:(h + 1) * HEAD_W]

    put(us_ref, S_FX, proj(W_FX))
    put(ur_ref, R_FG, _silu(proj(W_FG)))
    put(us_ref, S_PX, jnp.dot(h_ref[...], wpx_ref[...], preferred_element_type=F32))
    put(ur_ref, R_PG, _silu(proj(W_PG)))
    q = proj(W_Q)
    k = proj(W_K)
    if prompt:
        put_cache(kc_ref, k)
        put(ur_ref, R_Q, q * Q_SCALE)
        put(us_ref, S_K, k)
    else:
        cos, sa, sb = cos_ref[...], sa_ref[...], sb_ref[...]
        for h in range(N_ATT_HEADS):
            hc = slice(h * HEAD_W, (h + 1) * HEAD_W)
            put(ur_ref, R_Q + h * HEAD_W, _rope(q[:, hc], cos, sa, sb) * Q_SCALE)
            put(us_ref, S_K + h * HEAD_W, _rope(k[:, hc], cos, sa, sb))
    v = proj(W_V)
    if prompt:
        put_cache(vc_ref, v)
    put(us_ref, S_V, v)
    put(ur_ref, R_AG, _silu(proj(W_AG)))
    put(us_ref, S_GLU, proj(W_CA) * _sigmoid(proj(W_CB)))
    put(ur_ref, R_CG, _silu(proj(W_CG)))


def _inproj(x, mod4, norm_g3, w_in_b, w_px, l, prompt, rope_tables=None, caches=None, casts=()):
    t = x.shape[0]
    tm = TOKEN_TILE
    steps = t // tm
    if prompt:
        row = lambda i: 0
    else:
        row = lambda i: 1 + (i * tm) // DEC_SEQ
    in_specs = [pl.BlockSpec((tm, D_MODEL), lambda i: (i, 0)),
                pl.BlockSpec((None, None, 1, 3 * D_MODEL), lambda i: (l, row(i), 0, 0)),
                pl.BlockSpec((None, 1, D_MODEL), lambda i: (l, 0, 0)),
                _resident((D_MODEL, D_IN_PROJ), lambda i: (0, 0)),
                _resident((None, D_MODEL, D_BRANCH), lambda i: (l, 0, 0))]
    args = [x, mod4, norm_g3, w_in_b, w_px]
    in_specs += [pl.BlockSpec((None, w.shape[1] // steps, w.shape[2]), lambda i, wl=wl: (wl, i, 0))
                 for w, wl in casts]
    args += [w for w, _ in casts]
    act = jax.ShapeDtypeStruct((t, D_ACT), BF16)
    act_spec = pl.BlockSpec((tm, D_ACT), lambda i: (i, 0))
    out_shape = [act, act]
    out_specs = [act_spec, act_spec]
    aliases = {}
    if prompt:
        cache = jax.ShapeDtypeStruct((BATCH, DEPTH, N_ATT_HEADS, SEQ, HEAD_W), F32)
        cache_spec = pl.BlockSpec((tm // SEQ, None, N_ATT_HEADS, SEQ, HEAD_W), lambda i: (i, l, 0, 0, 0))
        out_shape += [cache, cache]
        out_specs += [cache_spec, cache_spec]
        if caches is not None:
            aliases = {len(args): 2, len(args) + 1: 3}
            in_specs += [pl.BlockSpec(memory_space=pl.ANY)] * 2
            args += list(caches)
    else:
        nblk = DEC_SEQ // tm
        in_specs += [pl.BlockSpec((tm, HEAD_W), lambda i: (i % nblk, 0))] * 3
        args += list(rope_tables)
    out_shape += [jax.ShapeDtypeStruct(w.shape[1:], BF16) for w, _ in casts]
    out_specs += [pl.BlockSpec((w.shape[1] // steps, w.shape[2]), lambda i: (i, 0)) for w, _ in casts]
    return pl.pallas_call(
        functools.partial(_inproj_kernel, prompt=prompt, aliased_caches=caches is not None,
                          n_cast=len(casts)),
        out_shape=out_shape,
        grid=(steps,),
        in_specs=in_specs,
        out_specs=out_specs,
        scratch_shapes=[pltpu.VMEM((tm, D_MODEL), BF16)],
        input_output_aliases=aliases,
        compiler_params=pltpu.CompilerParams(dimension_semantics=("arbitrary",), vmem_limit_bytes=VMEM_LIMIT),
        name="inproj_ctx" if prompt else "inproj_lat",
    )(*args)


def _make_mixer_kernel(s, l, with_past, final):
    lam_init = 0.8 - 0.6 * math.exp(-0.3 * l)
    past = PAST_LEN if with_past else 0
    sk = s + past
    nrb = s // ROW_BLOCK
    rb_rows = ROW_BLOCK
    n_out_chunks = D_MODEL // D_BRANCH
    nseq = max(STEP_BLOCKS // nrb, 1)
    steps_per_seq = max(nrb // STEP_BLOCKS, 1)

    def kern(*refs):
        it = iter(refs)
        us_ref, ur_ref, x_ref, mod_ref, wo_ref, fg_ref = (next(it) for _ in range(6))
        cs_ref, wf_ref, tp_ref, ps_ref = (next(it) for _ in range(4))
        dl_ref, sg_ref, fwd_ref, inv_ref, spec_ref, dwb_ref, lng_ref, lnb_ref, wpw_ref = (
            next(it) for _ in range(9))
        if with_past:
            ck_ref, cv_ref = next(it), next(it)
        o_ref = next(it)
        ppad_ref, gpad_ref, kall_ref, vx_ref, z_ref = (next(it) for _ in range(5))

        n = pl.program_id(0)

        def cols(c0, g=0, w=D_BRANCH):
            return slice(c0 + g * w, c0 + (g + 1) * w)

        def rows_at(base, size):
            return pl.ds(base, size) if nrb > 1 else slice(base, base + size)

        def stage_sequences():
            ones = jnp.ones((sk, HEAD_W), BF16)
            for q in range(nseq):
                seq_rows = slice(q * s, (q + 1) * s)
                for pad_ref, col0 in ((ppad_ref, S_PX), (gpad_ref, S_GLU)):
                    pad_ref[q, 0:CONV_PAD, :] = jnp.zeros((CONV_PAD, D_BRANCH), BF16)
                    pad_ref[q, CONV_PAD + s:CONV_PAD + s + CONV_TAIL, :] = jnp.zeros((CONV_TAIL, D_BRANCH), BF16)
                    pad_ref[q, CONV_PAD:CONV_PAD + s, :] = us_ref[seq_rows, cols(col0)]
                for h in range(N_ATT_HEADS):
                    if with_past:
                        kall_ref[q, h, 0:past, :] = ck_ref[h].astype(BF16)
                        vx_ref[q, h, 0:past, 0:HEAD_W] = cv_ref[h].astype(BF16)
                    kall_ref[q, h, past:sk, :] = us_ref[seq_rows, cols(S_K, h, HEAD_W)]
                    vx_ref[q, h, past:sk, 0:HEAD_W] = us_ref[seq_rows, cols(S_V, h, HEAD_W)]
                    vx_ref[q, h, :, HEAD_W:2 * HEAD_W] = ones

        if steps_per_seq == 1:
            stage_sequences()
        else:
            pl.when(n % steps_per_seq == 0)(stage_sequences)

        def block_place(j):
            if nrb == 1:
                return j, 0, slice(j * rb_rows, (j + 1) * rb_rows)
            r0 = pl.multiple_of(((n % steps_per_seq) * STEP_BLOCKS + j) * rb_rows, rb_rows)
            return 0, r0, slice(j * rb_rows, (j + 1) * rb_rows)

        def out_projection():
            step_rows = STEP_BLOCKS * rb_rows
            for c in range(n_out_chunks):
                y = jnp.dot(z_ref[...], wo_ref[:, cols(0, c)], preferred_element_type=F32)
                gate = mod_ref[:, 2 * D_MODEL + c * D_BRANCH:2 * D_MODEL + (c + 1) * D_BRANCH]
                o_ref[:, cols(0, c)] = x_ref[:, cols(0, c)] + gate * y
            if final:
                ssq = jnp.zeros((step_rows, 1), F32)
                for c in range(n_out_chunks):
                    out = o_ref[:, cols(0, c)]
                    ssq = ssq + jnp.sum(out * out, axis=-1, keepdims=True)
                inv = lax.rsqrt(ssq * (1.0 / D_MODEL) + EPS)
                for c in range(n_out_chunks):
                    o_ref[:, cols(0, c)] = o_ref[:, cols(0, c)] * inv * fg_ref[:, cols(0, c)]

        def fourier(j):
            q, r0, rows = block_place(j)
            fx = us_ref[q * s:(q + 1) * s, cols(S_FX)]
            pr = jnp.dot(cs_ref[rows_at(r0, rb_rows), :], fx, preferred_element_type=F32)
            pi = jnp.dot(cs_ref[rows_at(s + r0, rb_rows), :], fx, preferred_element_type=F32)
            yf = (jnp.dot(pr.astype(BF16), wf_ref[0:D_BRANCH, :], preferred_element_type=F32)
                  + jnp.dot(pi.astype(BF16), wf_ref[D_BRANCH:2 * D_BRANCH, :], preferred_element_type=F32))
            z_ref[rows, cols(0)] = (yf * ur_ref[rows, cols(R_FG)].astype(F32)).astype(BF16)

        def pooling(j):
            q, r0, rows = block_place(j)
            for k in range(rb_rows // CONV_OUT):
                base = r0 + k * CONV_OUT
                t = base + lax.broadcasted_iota(jnp.int32, (CONV_OUT, POOL_GROUP), 0)
                out_rows = slice(rows.start + k * CONV_OUT, rows.start + (k + 1) * CONV_OUT)
                for g, w in enumerate(POOL_WINDOWS):
                    win = ppad_ref[q, rows_at(base, CONV_DFT), cols(0, g, POOL_GROUP)]
                    ssum = jnp.dot(tp_ref[g], win, preferred_element_type=F32)
                    cnt = (jnp.minimum(t + w // 2, s) - jnp.maximum(t - w // 2, 0)).astype(F32)
                    px = ppad_ref[q, rows_at(base + CONV_PAD, CONV_OUT), cols(0, g, POOL_GROUP)].astype(F32)
                    pg = ur_ref[out_rows, cols(R_PG, g, POOL_GROUP)].astype(F32)
                    z_ref[out_rows, cols(D_BRANCH, g, POOL_GROUP)] = (
                        (ssum / cnt - px) * ps_ref[:, cols(0, g, POOL_GROUP)] * pg).astype(BF16)

        dl = dl_ref[...]
        lam = (jnp.exp(jnp.sum(dl[0:1] * dl[1:2], axis=-1, keepdims=True))
               - jnp.exp(jnp.sum(dl[2:3] * dl[3:4], axis=-1, keepdims=True)) + lam_init)
        first_map = lax.broadcasted_iota(jnp.int32, (rb_rows, HEAD_W), 1) < DIFF_HEAD

        def attention(h, j):
            q, _, rows = block_place(j)
            qf = ur_ref[rows, cols(R_Q, h, HEAD_W)].astype(F32)
            qq = jnp.concatenate([jnp.where(first_map, qf, 0.0).astype(BF16),
                                  jnp.where(first_map, 0.0, qf).astype(BF16)], axis=0)
            sc = lax.dot_general(qq, kall_ref[q, h], (((1,), (1,)), ((), ())), preferred_element_type=F32)
            m = jnp.max(sc, axis=-1, keepdims=True)
            p = jnp.exp2((sc - m).astype(BF16))
            ov = jnp.dot(p, vx_ref[q, h], preferred_element_type=F32)
            o = (ov[0:rb_rows, 0:HEAD_W] / ov[0:rb_rows, HEAD_W:2 * HEAD_W]
                 - lam * (ov[rb_rows:2 * rb_rows, 0:HEAD_W] / ov[rb_rows:2 * rb_rows, HEAD_W:2 * HEAD_W]))
            o = o * lax.rsqrt(jnp.mean(o * o, axis=-1, keepdims=True) + EPS) * sg_ref[...] * (1.0 - lam_init)
            ag = ur_ref[rows, cols(R_AG, h, HEAD_W)].astype(F32)
            z_ref[rows, cols(2 * D_BRANCH, h, HEAD_W)] = (o * ag).astype(BF16)

        def conv_module(j):
            q, r0, rows = block_place(j)
            half = CONV_DFT // 2
            parts = []
            for k in range(rb_rows // CONV_OUT):
                win = gpad_ref[q, rows_at(r0 + k * CONV_OUT, CONV_DFT), :]
                spec = jnp.dot(fwd_ref[...], win, preferred_element_type=F32)
                yr, yi = spec[0:half], spec[half:CONV_DFT]
                zr = yr * spec_ref[0] - yi * spec_ref[1]
                zi = yr * spec_ref[1] + yi * spec_ref[2]
                z = jnp.concatenate([zr, zi], axis=0).astype(BF16)
                parts.append(jnp.dot(inv_ref[...], z, preferred_element_type=F32))
            y = jnp.concatenate(parts, axis=0) + dwb_ref[...]
            mu = jnp.mean(y, axis=-1, keepdims=True)
            d = y - mu
            var = jnp.mean(d * d, axis=-1, keepdims=True)
            act = _silu(d * lax.rsqrt(var + EPS) * lng_ref[...] + lnb_ref[...]).astype(BF16)
            yc = jnp.dot(act, wpw_ref[...], preferred_element_type=F32)
            z_ref[rows, cols(3 * D_BRANCH)] = (yc * ur_ref[rows, cols(R_CG)].astype(F32)).astype(BF16)

        pieces = [functools.partial(attention, h) for h in range(N_ATT_HEADS)] + [conv_module, fourier, pooling]
        for piece in pieces:
            for j in range(STEP_BLOCKS):
                piece(j)
        out_projection()

    return kern


def _mixer(us, ur, x, mod4, w_out_b, final_g2, consts, params, l, s, prompt, final, cache_k=None, cache_v=None):
    with_past = cache_k is not None
    nrb = s // ROW_BLOCK
    n_blocks = x.shape[0] // ROW_BLOCK
    sk = s + (PAST_LEN if with_past else 0)
    wfold, pool_scale3, diff_lambda, subln3, conv_spec, dwb3, lng3, lnb3, w_pw_b = params

    step_rows = STEP_BLOCKS * ROW_BLOCK
    nseq = max(STEP_BLOCKS // nrb, 1)
    steps_per_seq = max(nrb // STEP_BLOCKS, 1)
    assert n_blocks % STEP_BLOCKS == 0 and (nrb % STEP_BLOCKS == 0 or (nrb == 1 and prompt))

    def mod_row(n):
        return 0 if prompt else 1 + n // steps_per_seq

    def per_layer(shape):
        nd = len(shape)
        return pl.BlockSpec((None,) + shape, lambda n: (l,) + (0,) * nd)

    in_specs = [pl.BlockSpec((nseq * s, D_ACT), lambda n: (n // steps_per_seq, 0)),
                pl.BlockSpec((step_rows, D_ACT), lambda n: (n, 0)),
                pl.BlockSpec((step_rows, D_MODEL), lambda n: (n, 0)),
                pl.BlockSpec((None, None, 1, 3 * D_MODEL), lambda n: (l, mod_row(n), 0, 0)),
                _resident((D_MODEL, D_MODEL), lambda n: (0, 0)),
                pl.BlockSpec((1, D_MODEL), lambda n: (0, 0)),
                _resident((2 * s, s), lambda n: (0, 0)),
                _resident((None, 2 * D_BRANCH, D_BRANCH), lambda n: (l, 0, 0)),
                _resident((len(POOL_WINDOWS), CONV_OUT, CONV_DFT), lambda n: (0, 0, 0)),
                per_layer((1, D_BRANCH)),
                per_layer((4, DIFF_HEAD)),
                per_layer((1, HEAD_W)),
                _resident((CONV_DFT, CONV_DFT), lambda n: (0, 0)),
                _resident((CONV_OUT, CONV_DFT), lambda n: (0, 0)),
                _resident((None, 3, CONV_DFT // 2, D_BRANCH), lambda n: (l, 0, 0, 0)),
                per_layer((1, D_BRANCH)),
                per_layer((1, D_BRANCH)),
                per_layer((1, D_BRANCH)),
                _resident((None, D_BRANCH, D_BRANCH), lambda n: (l, 0, 0))]
    args = [us, ur, x, mod4, w_out_b, final_g2, consts["cs"], wfold, consts["tpool"], pool_scale3,
            diff_lambda, subln3, consts["conv_fwd"], consts["conv_inv"], conv_spec, dwb3, lng3, lnb3, w_pw_b]
    if with_past:
        cache_spec = pl.BlockSpec((None, None, N_ATT_HEADS, PAST_LEN, HEAD_W),
                                  lambda n: (n // steps_per_seq, l, 0, 0, 0))
        in_specs += [cache_spec, cache_spec]
        args += [cache_k, cache_v]
    scratch = [pltpu.VMEM((nseq, CONV_PAD + s + CONV_TAIL, D_BRANCH), BF16),
               pltpu.VMEM((nseq, CONV_PAD + s + CONV_TAIL, D_BRANCH), BF16),
               pltpu.VMEM((nseq, N_ATT_HEADS, sk, HEAD_W), BF16),
               pltpu.VMEM((nseq, N_ATT_HEADS, sk, 2 * HEAD_W), BF16),
               pltpu.VMEM((step_rows, D_MODEL), BF16)]
    return pl.pallas_call(
        _make_mixer_kernel(s, l, with_past, final),
        out_shape=jax.ShapeDtypeStruct(x.shape, F32),
        grid=(n_blocks // STEP_BLOCKS,),
        in_specs=in_specs,
        out_specs=pl.BlockSpec((step_rows, D_MODEL), lambda n: (n, 0)),
        scratch_shapes=scratch,
        compiler_params=pltpu.CompilerParams(dimension_semantics=("arbitrary",), vmem_limit_bytes=VMEM_LIMIT),
        name="mixer_ctx" if prompt else "mixer_lat",
    )(*args)


def kernel(x_prompt, x_sample, cache_k, cache_v, c, c_ctx, norm_g, w_mod, b_mod, w_in, w_fourier, w_pool,
           pool_scale, diff_lambda, subln_g, conv_dw, conv_dw_b, conv_ln_g, conv_ln_b, w_conv_pw, w_out, final_g):
    w_in_b = w_in[0].astype(BF16)
    w_pw_b = w_conv_pw.astype(BF16)
    w_px = _fold_pool(w_in, w_pool)
    wfold = _fold_fourier(w_fourier)

    cv = jnp.concatenate([c_ctx[None, :], c, jnp.zeros((MOD_ROWS - 1 - DEC_BATCH, D_MODEL), F32)], axis=0)
    mod4 = _modulation(cv, w_mod, b_mod).reshape(DEPTH, MOD_ROWS, 1, 3 * D_MODEL)

    norm_g3 = norm_g.reshape(DEPTH, 1, D_MODEL)
    final_g2 = final_g.reshape(1, D_MODEL)
    params = (wfold, pool_scale.reshape(DEPTH, 1, D_BRANCH), diff_lambda,
              subln_g.reshape(DEPTH, 1, HEAD_W),
              _conv_spectrum(conv_dw),
              conv_dw_b.reshape(DEPTH, 1, D_BRANCH),
              conv_ln_g.reshape(DEPTH, 1, D_BRANCH), conv_ln_b.reshape(DEPTH, 1, D_BRANCH), w_pw_b)

    tpool = jnp.asarray(_pool_windows()).astype(BF16)
    conv_fwd, conv_inv = (jnp.asarray(m).astype(BF16) for m in _conv_dft())
    shared = {"tpool": tpool, "conv_fwd": conv_fwd, "conv_inv": conv_inv}
    consts_ctx = dict(shared, cs=jnp.asarray(_dft_positions(SEQ)).astype(BF16))
    consts_lat = dict(shared, cs=jnp.asarray(_dft_positions(DEC_SEQ)).astype(BF16))
    rope_tables = tuple(jnp.asarray(t) for t in _rope_tables(DEC_SEQ))

    xp = x_prompt.reshape(BATCH * SEQ, D_MODEL)
    xs = x_sample.reshape(DEC_BATCH * DEC_SEQ, D_MODEL)
    caches = None
    for l in range(DEPTH):
        final = l == DEPTH - 1
        casts = ((w_out, l),) if final else ((w_out, l), (w_in, l + 1))
        us, ur, kc, vc, w_out_b, *next_w_in = _inproj(xp, mod4, norm_g3, w_in_b, w_px, l, prompt=True,
                                                      caches=caches, casts=casts)
        caches = (kc, vc)
        xp = _mixer(us, ur, xp, mod4, w_out_b, final_g2, consts_ctx, params, l, SEQ, prompt=True, final=final)

        us, ur = _inproj(xs, mod4, norm_g3, w_in_b, w_px, l, prompt=False, rope_tables=rope_tables)
        xs = _mixer(us, ur, xs, mod4, w_out_b, final_g2, consts_lat, params, l, DEC_SEQ, prompt=False,
                    final=final, cache_k=cache_k, cache_v=cache_v)
        if next_w_in:
            (w_in_b,) = next_w_in

    return (xp.reshape(BATCH, SEQ, D_MODEL), xs.reshape(DEC_BATCH, DEC_SEQ, D_MODEL), caches[0], caches[1])
```
